```python
import math
import jax, jax.numpy as jnp
from jax import lax
import numpy as np


D_MODEL = 1024
BATCH = 16
SEQ = 4096
DEPTH = 2
DEC_BATCH = 8
DEC_SEQ = 4096
PAST_LEN = 128

N_MIXERS = 2
N_RET_LAYERS = (DEPTH + 1) // 2
N_ATT_LAYERS = DEPTH // 2
GRID_W = 64
ROPE_THETA = 10000.0
EPS = 1e-6
RET_HEADS = 4
RET_DK = D_MODEL // RET_HEADS
RET_DV = 2 * D_MODEL // RET_HEADS
RET_CHUNK = 128
RET_IN = 2 * RET_HEADS * RET_DK + 2 * RET_HEADS * RET_DV
ATT_HD = 128
ATT_Q_HEADS = D_MODEL // ATT_HD
ATT_KV_HEADS = 2
ATT_GROUP = ATT_Q_HEADS // ATT_KV_HEADS
ATT_IN = (ATT_Q_HEADS + 2 * ATT_KV_HEADS) * ATT_HD
ATT_BLOCK = 128
D_FF = 4 * D_MODEL

kernel_name = 'hybrid_retention_gqa_adaln_encoder'


def rmsnorm(x, g):
    xf = x.astype(jnp.float32)
    y = xf * lax.rsqrt(jnp.mean(xf * xf, axis=-1, keepdims=True) + EPS)
    return (y * g.astype(jnp.float32)).astype(x.dtype)


def axial_rope_tables(n_tokens, head_dim):
    rows = n_tokens // GRID_W
    row = jnp.repeat(jnp.arange(rows, dtype=jnp.float32), GRID_W)
    col = jnp.tile(jnp.arange(GRID_W, dtype=jnp.float32), rows)
    nf = head_dim // 4
    inv = ROPE_THETA ** (-jnp.arange(nf, dtype=jnp.float32) / nf)
    ar = row[:, None] * inv[None, :]
    ac = col[:, None] * inv[None, :]
    return (jnp.cos(ar), jnp.sin(ar), jnp.cos(ac), jnp.sin(ac))


def _rot(seg, cos, sin):
    a, b = jnp.split(seg, 2, axis=-1)
    cos = cos[:, None, :]
    sin = sin[:, None, :]
    return jnp.concatenate([a * cos - b * sin, a * sin + b * cos], axis=-1)


def apply_axial_rope(x, tables):
    cos_r, sin_r, cos_c, sin_c = tables
    xf = x.astype(jnp.float32)
    half = xf.shape[-1] // 2
    return jnp.concatenate([_rot(xf[..., :half], cos_r, sin_r),
                            _rot(xf[..., half:], cos_c, sin_c)], axis=-1)


def adaln(c, w, b):
    mod = jax.nn.silu(c.astype(jnp.float32)) @ w.astype(jnp.float32) + b.astype(jnp.float32)
    return [m[:, None, :] for m in jnp.split(mod, 6, axis=-1)]


def retention(h, w_in, decay_logit, w_out, rope):
    b, n, _ = h.shape
    f32 = jnp.float32
    proj = h @ w_in
    q, k, v, g = jnp.split(proj, [RET_HEADS * RET_DK, 2 * RET_HEADS * RET_DK,
                                  2 * RET_HEADS * RET_DK + RET_HEADS * RET_DV], axis=-1)
    q = apply_axial_rope(q.reshape(b, n, RET_HEADS, RET_DK), rope) * (RET_DK ** -0.5)
    k = apply_axial_rope(k.reshape(b, n, RET_HEADS, RET_DK), rope)
    v = v.reshape(b, n, RET_HEADS, RET_DV).astype(f32)
    nc = n // RET_CHUNK

    def to_chunks(t):
        return t.reshape(b, nc, RET_CHUNK, RET_HEADS, t.shape[-1]).transpose(1, 0, 3, 2, 4)

    qc, kc, vc = to_chunks(q), to_chunks(k), to_chunks(v)
    log_g = -jax.nn.softplus(-decay_logit.astype(f32))
    lf, lb = log_g[0], log_g[1]
    idx = jnp.arange(RET_CHUNK, dtype=f32)
    diff = idx[:, None] - idx[None, :]
    d_fwd = jnp.where(diff >= 0, jnp.exp(lf[:, None, None] * jnp.maximum(diff, 0.0)), 0.0)
    d_bwd = jnp.where(diff < 0, jnp.exp(lb[:, None, None] * jnp.maximum(-diff, 0.0)), 0.0)
    scores = jnp.einsum('nbhjd,nbhld->nbhjl', qc, kc) * (d_fwd + d_bwd)
    inner = jnp.einsum('nbhjl,nbhle->nbhje', scores, vc)

    qdec_f = jnp.exp(lf[:, None] * (idx + 1.0))[:, :, None]
    kdec_f = jnp.exp(lf[:, None] * (RET_CHUNK - 1.0 - idx))[:, :, None]
    cdec_f = jnp.exp(lf * RET_CHUNK)[:, None, None]
    qdec_b = jnp.exp(lb[:, None] * (RET_CHUNK - idx))[:, :, None]
    kdec_b = jnp.exp(lb[:, None] * idx)[:, :, None]
    cdec_b = jnp.exp(lb * RET_CHUNK)[:, None, None]

    def fwd_step(state, xs):
        qi, ki, vi = xs
        out = jnp.einsum('bhjd,bhde->bhje', qi * qdec_f, state)
        state = state * cdec_f + jnp.einsum('bhld,bhle->bhde', ki * kdec_f, vi)
        return state, out

    def bwd_step(state, xs):
        qi, ki, vi = xs
        out = jnp.einsum('bhjd,bhde->bhje', qi * qdec_b, state)
        state = state * cdec_b + jnp.einsum('bhld,bhle->bhde', ki * kdec_b, vi)
        return state, out

    state0 = jnp.zeros((b, RET_HEADS, RET_DK, RET_DV), f32)
    _, cross_f = lax.scan(fwd_step, state0, (qc, kc, vc))
    _, cross_b = lax.scan(bwd_step, state0, (qc, kc, vc), reverse=True)
    y = (inner + cross_f + cross_b).transpose(1, 0, 3, 2, 4).reshape(b, n, RET_HEADS, RET_DV)
    mu = jnp.mean(y, axis=-1, keepdims=True)
    var = jnp.mean(jnp.square(y - mu), axis=-1, keepdims=True)
    y = (y - mu) * lax.rsqrt(var + EPS)
    y = y.reshape(b, n, RET_HEADS * RET_DV) * jax.nn.silu(g.astype(f32))
    return y.astype(h.dtype) @ w_out


def attention(h, w_in, q_gain, k_gain, w_out, rope):
    b, n, _ = h.shape
    proj = h @ w_in
    q, k, v = jnp.split(proj, [ATT_Q_HEADS * ATT_HD, (ATT_Q_HEADS + ATT_KV_HEADS) * ATT_HD], axis=-1)
    q = apply_axial_rope(rmsnorm(q.reshape(b, n, ATT_Q_HEADS, ATT_HD), q_gain), rope) * (ATT_HD ** -0.5)
    k = apply_axial_rope(rmsnorm(k.reshape(b, n, ATT_KV_HEADS, ATT_HD), k_gain), rope)
    v = v.reshape(b, n, ATT_KV_HEADS, ATT_HD)
    qb_all = q.reshape(b, n // ATT_BLOCK, ATT_BLOCK, ATT_KV_HEADS, ATT_GROUP, ATT_HD).swapaxes(0, 1)

    def block(qb):
        s = jnp.einsum('bqhgd,bkhd->bhgqk', qb, k)
        p = jax.nn.softmax(s.astype(jnp.float32), axis=-1)
        return jnp.einsum('bhgqk,bkhd->bqhgd', p.astype(v.dtype), v)

    o = lax.map(block, qb_all)
    o = o.swapaxes(0, 1).reshape(b, n, ATT_Q_HEADS * ATT_HD)
    return o.astype(h.dtype) @ w_out


def mlp(h, w1, w2):
    return jnp.square(jax.nn.relu(h @ w1)) @ w2


def trunk(x, c, mod_w, mod_b, norm1_g, norm2_g, ret_w_in, ret_decay, ret_w_out,
          att_w_in, att_q_gain, att_k_gain, att_w_out, mlp_w1, mlp_w2, final_g):
    n = x.shape[1]
    rope_ret = axial_rope_tables(n, RET_DK)
    rope_att = axial_rope_tables(n, ATT_HD)
    for i in range(DEPTH):
        sh1, sc1, g1, sh2, sc2, g2 = adaln(c, mod_w[i], mod_b[i])
        h = (rmsnorm(x, norm1_g[i]) * (1.0 + sc1) + sh1).astype(x.dtype)
        j = i // N_MIXERS
        if i % N_MIXERS == 0:
            m = retention(h, ret_w_in[j], ret_decay[j], ret_w_out[j], rope_ret)
        else:
            m = attention(h, att_w_in[j], att_q_gain[j], att_k_gain[j], att_w_out[j], rope_att)
        x = (x + g1 * m).astype(x.dtype)
        h = (rmsnorm(x, norm2_g[i]) * (1.0 + sc2) + sh2).astype(x.dtype)
        x = (x + g2 * mlp(h, mlp_w1[i], mlp_w2[i])).astype(x.dtype)
    return rmsnorm(x, final_g)


def setup_inputs(seed: int = 0) -> dict:
    key = jax.random.key(seed)
    ks = jax.random.split(key, 18)
    f32 = jnp.float32

    def nrm(k, shape, fan_in):
        return jax.random.normal(k, shape, f32) * (fan_in ** -0.5)

    gam_f = 1.0 - 2.0 ** (-5.0 - np.arange(RET_HEADS))
    gam = np.stack([gam_f, gam_f[::-1]])
    base = jnp.asarray(np.log(gam / (1.0 - gam)), f32)
    ret_decay = base[None] + 0.1 * jax.random.normal(ks[9], (N_RET_LAYERS, 2, RET_HEADS), f32)
    return {
        'x_prompt': jax.random.normal(ks[0], (BATCH, SEQ, D_MODEL), f32),
        'x_sample': jax.random.normal(ks[1], (DEC_BATCH, DEC_SEQ, D_MODEL), f32),
        'c_prompt': jax.random.normal(ks[2], (BATCH, D_MODEL), f32),
        'c_sample': jax.random.normal(ks[3], (DEC_BATCH, D_MODEL), f32),
        'mod_w': nrm(ks[4], (DEPTH, D_MODEL, 6 * D_MODEL), D_MODEL),
        'mod_b': 0.02 * jax.random.normal(ks[5], (DEPTH, 6 * D_MODEL), f32),
        'norm1_g': 1.0 + 0.05 * jax.random.normal(ks[6], (DEPTH, D_MODEL), f32),
        'norm2_g': 1.0 + 0.05 * jax.random.normal(ks[7], (DEPTH, D_MODEL), f32),
        'ret_w_in': nrm(ks[8], (N_RET_LAYERS, D_MODEL, RET_IN), D_MODEL),
        'ret_decay': ret_decay,
        'ret_w_out': nrm(ks[10], (N_RET_LAYERS, RET_HEADS * RET_DV, D_MODEL), RET_HEADS * RET_DV),
        'att_w_in': nrm(ks[11], (N_ATT_LAYERS, D_MODEL, ATT_IN), D_MODEL),
        'att_q_gain': 1.0 + 0.05 * jax.random.normal(ks[12], (N_ATT_LAYERS, ATT_HD), f32),
        'att_k_gain': 1.0 + 0.05 * jax.random.normal(ks[13], (N_ATT_LAYERS, ATT_HD), f32),
        'att_w_out': nrm(ks[14], (N_ATT_LAYERS, ATT_Q_HEADS * ATT_HD, D_MODEL), ATT_Q_HEADS * ATT_HD),
        'mlp_w1': nrm(ks[15], (DEPTH, D_MODEL, D_FF), D_MODEL),
        'mlp_w2': nrm(ks[16], (DEPTH, D_FF, D_MODEL), D_FF),
        'final_g': 1.0 + 0.05 * jax.random.normal(ks[17], (D_MODEL,), f32),
    }


def reference(x_prompt, x_sample, c_prompt, c_sample, mod_w, mod_b, norm1_g, norm2_g,
              ret_w_in, ret_decay, ret_w_out, att_w_in, att_q_gain, att_k_gain, att_w_out,
              mlp_w1, mlp_w2, final_g):
    y_prompt = trunk(x_prompt, c_prompt, mod_w, mod_b, norm1_g, norm2_g, ret_w_in, ret_decay,
                     ret_w_out, att_w_in, att_q_gain, att_k_gain, att_w_out, mlp_w1, mlp_w2, final_g)
    y_sample = trunk(x_sample, c_sample, mod_w, mod_b, norm1_g, norm2_g, ret_w_in, ret_decay,
                     ret_w_out, att_w_in, att_q_gain, att_k_gain, att_w_out, mlp_w1, mlp_w2, final_g)
    return (y_prompt, y_sample)
```

```python
import functools

import jax
import jax.numpy as jnp
from jax import lax
from jax.experimental import pallas as pl
from jax.experimental.pallas import tpu as pltpu

F32 = jnp.float32
BF16 = jnp.bfloat16

GRID_W = 64
ROPE_THETA = 10000.0
EPS = 1e-6
RET_HEADS = 4
ATT_HD = 128
ATT_KV_HEADS = 2
N_MOD = 6

V7X_VMEM_BYTES = 64 * 1024 * 1024
V7X_LANES = 128
V7X_MXU_DIM = 256

VMEM_LIMIT = V7X_VMEM_BYTES - 8 * 1024 * 1024
TOK_TILE = 1024
MLP_TOK_TILE = 512
FF_CHUNK = 1024
RET_CHUNK = V7X_MXU_DIM
ATT_Q_TILE = 512
ATT_K_BLOCK = 512


def _params(*sem):
    return pltpu.CompilerParams(dimension_semantics=sem, vmem_limit_bytes=VMEM_LIMIT)


def _resident(shape):
    nd = len(shape)
    return pl.BlockSpec(shape, lambda *_: (0,) * nd, pipeline_mode=pl.Buffered(1))


def _rms(x):
    return x * lax.rsqrt(jnp.mean(x * x, axis=-1, keepdims=True) + EPS)


def _dot(a, b):
    return jnp.dot(a, b, preferred_element_type=F32)


def _adaln_kernel(c_ref, w_ref, b_ref, o_ref):
    c = c_ref[...]
    s = c * jax.nn.sigmoid(c)
    w = w_ref[0]
    s_hi = s.astype(BF16)
    s_lo = (s - s_hi.astype(F32)).astype(BF16)
    w_hi = w.astype(BF16)
    w_lo = (w - w_hi.astype(F32)).astype(BF16)
    o_ref[0] = _dot(s_hi, w_hi) + _dot(s_lo, w_hi) + _dot(s_hi, w_lo) + b_ref[0]


def _adaln(c_all, mod_w, mod_b):
    depth, d, n6 = mod_w.shape
    rows = c_all.shape[0]
    tn = n6 // 4
    return pl.pallas_call(
        _adaln_kernel,
        grid=(depth, n6 // tn),
        in_specs=[
            pl.BlockSpec((rows, d), lambda i, j: (0, 0)),
            pl.BlockSpec((1, d, tn), lambda i, j: (i, 0, j)),
            pl.BlockSpec((1, 1, tn), lambda i, j: (i, 0, j)),
        ],
        out_specs=pl.BlockSpec((1, rows, tn), lambda i, j: (i, 0, j)),
        out_shape=jax.ShapeDtypeStruct((depth, rows, n6), F32),
        compiler_params=_params("arbitrary", "arbitrary"),
        name="adaln_mod",
    )(c_all, mod_w, mod_b.reshape(depth, 1, n6))


def _modulated_norm(x, gain, shift, scale):
    return (_rms(x) * gain * (1.0 + scale) + shift).astype(BF16)


def _ret_in_kernel(x_ref, mod_ref, g_ref, w_ref, cos_ref, sin_ref, o_ref, h_scr, *, d):
    j = pl.program_id(2)
    hw = 2 * V7X_LANES

    @pl.when(j == 0)
    def _():
        m = mod_ref[0]
        h_scr[...] = _modulated_norm(x_ref[0], g_ref[...], m[0:1], m[1:2])

    @pl.when(j == 0)
    def _():
        cos = cos_ref[...]
        sin = sin_ref[...]
        h = h_scr[...]
        for hh in range(2 * d // hw):
            acc = _dot(h, w_ref[:, hh * hw:(hh + 1) * hw])
            if hh < RET_HEADS:
                acc = acc * (hw ** -0.5)
            a = acc[:, :V7X_LANES]
            b = acc[:, V7X_LANES:]
            o_ref[0, :, hh * hw:hh * hw + V7X_LANES] = (a * cos - b * sin).astype(BF16)
            o_ref[0, :, hh * hw + V7X_LANES:(hh + 1) * hw] = (a * sin + b * cos).astype(BF16)

    @pl.when(j == 1)
    def _():
        h = h_scr[...]
        for cc in range(2 * d // 512):
            cs = slice(cc * 512, (cc + 1) * 512)
            o_ref[0, :, cs] = _dot(h, w_ref[:, cs]).astype(BF16)

    @pl.when(j == 2)
    def _():
        h = h_scr[...]
        for cc in range(2 * d // 512):
            cs = slice(cc * 512, (cc + 1) * 512)
            g = _dot(h, w_ref[:, cs])
            o_ref[0, :, cs] = (g * jax.nn.sigmoid(g)).astype(BF16)


def _ret_in(x, mod, gain, w, cos, sin):
    b, n, d = x.shape
    tm = TOK_TILE
    sec = 2 * d
    return pl.pallas_call(
        functools.partial(_ret_in_kernel, d=d),
        grid=(b, n // tm, w.shape[1] // sec),
        in_specs=[
            pl.BlockSpec((1, tm, d), lambda bi, i, j: (bi, i, 0)),
            pl.BlockSpec((1, N_MOD, d), lambda bi, i, j: (bi, 0, 0)),
            pl.BlockSpec((1, d), lambda bi, i, j: (0, 0)),
            pl.BlockSpec((d, sec), lambda bi, i, j: (0, j)),
            pl.BlockSpec((tm, V7X_LANES), lambda bi, i, j: (i, 0)),
            pl.BlockSpec((tm, V7X_LANES), lambda bi, i, j: (i, 0)),
        ],
        out_specs=pl.BlockSpec((1, tm, sec), lambda bi, i, j: (bi, i, j)),
        out_shape=jax.ShapeDtypeStruct((b, n, w.shape[1]), BF16),
        scratch_shapes=[pltpu.VMEM((tm, d), BF16)],
        compiler_params=_params("arbitrary", "arbitrary", "arbitrary"),
        name="ret_in_proj",
    )(x, mod, gain, w, cos, sin)


def _retention_kernel(q_ref, k_ref, v_ref, g_ref, dec_ref, o_ref,
                      sb_scr, st_scr, stb_scr, qdf_scr, kdf_scr, qdb_scr, kdb_scr, dm_scr, *, nc):
    c = RET_CHUNK
    dk = q_ref.shape[-1]
    dv = v_ref.shape[-1]
    dec = dec_ref[0]
    neg = -dec
    log_g = -(jnp.maximum(neg, 0.0) + jnp.log1p(jnp.exp(-jnp.abs(neg))))
    lf = log_g[0:1, :]
    lb = log_g[1:2, :]
    lf_k = lf[:, :dk]
    lb_k = lb[:, :dk]
    pos = lax.broadcasted_iota(jnp.int32, (c, dk), 0).astype(F32)
    qdf_scr[...] = jnp.exp(lf_k * (pos + 1.0))
    kdf_scr[...] = jnp.exp(lf_k * (c - 1.0 - pos))
    qdb_scr[...] = jnp.exp(lb_k * (c - pos))
    kdb_scr[...] = jnp.exp(lb_k * pos)
    diff = (lax.broadcasted_iota(jnp.int32, (c, c), 0)
            - lax.broadcasted_iota(jnp.int32, (c, c), 1)).astype(F32)
    dm_scr[...] = jnp.where(diff >= 0.0,
                            jnp.exp(lf[:, :c] * jnp.maximum(diff, 0.0)),
                            jnp.exp(lb[:, :c] * jnp.maximum(-diff, 0.0)))
    cdf = jnp.exp(lf[:, :dv] * c)
    cdb = jnp.exp(lb[:, :dv] * c)
    contract0 = (((0,), (0,)), ((), ()))

    def rows(i):
        return pl.ds(pl.multiple_of(i * c, c), c)

    st_scr[...] = jnp.zeros_like(st_scr)

    def bwd_body(t, carry):
        i = nc - 1 - t
        r = rows(i)
        sb_scr[i] = st_scr[...].astype(BF16)
        kb = (k_ref[0, r, :].astype(F32) * kdb_scr[...]).astype(BF16)
        upd = lax.dot_general(kb, v_ref[0, r, :], contract0, preferred_element_type=F32)
        st_scr[...] = st_scr[...] * cdb + upd
        return carry

    lax.fori_loop(0, nc - 1, bwd_body, 0)
    sb_scr[0] = st_scr[...].astype(BF16)

    st_scr[...] = jnp.zeros_like(st_scr)
    stb_scr[...] = jnp.zeros_like(stb_scr)

    def fwd_body(i, carry):
        r = rows(i)
        q = q_ref[0, r, :]
        k = k_ref[0, r, :]
        v = v_ref[0, r, :]
        s = lax.dot_general(q, k, (((1,), (1,)), ((), ())), preferred_element_type=F32)
        p = (s * dm_scr[...]).astype(BF16)
        q32 = q.astype(F32)
        qf = (q32 * qdf_scr[...]).astype(BF16)
        qb = (q32 * qdb_scr[...]).astype(BF16)
        y = _dot(p, v) + _dot(qf, stb_scr[...]) + _dot(qb, sb_scr[i])
        mu = jnp.mean(y, axis=-1, keepdims=True)
        yc = y - mu
        var = jnp.mean(yc * yc, axis=-1, keepdims=True)
        yn = yc * lax.rsqrt(var + EPS)
        o_ref[0, r, :] = (yn * g_ref[0, r, :].astype(F32)).astype(BF16)
        kf = (k.astype(F32) * kdf_scr[...]).astype(BF16)
        upd = lax.dot_general(kf, v, contract0, preferred_element_type=F32)
        st = st_scr[...] * cdf + upd
        st_scr[...] = st
        stb_scr[...] = st.astype(BF16)
        return carry

    lax.fori_loop(0, nc, fwd_body, 0)


def _retention(proj, dec, d):
    b, n, _ = proj.shape
    dk = d // RET_HEADS
    dv = 2 * d // RET_HEADS
    c = RET_CHUNK
    nc = n // c
    k_off = d // dk
    v_off = 2 * d // dv
    g_off = 4 * d // dv
    return pl.pallas_call(
        functools.partial(_retention_kernel, nc=nc),
        grid=(b, RET_HEADS),
        in_specs=[
            pl.BlockSpec((1, n, dk), lambda bi, h: (bi, 0, h)),
            pl.BlockSpec((1, n, dk), lambda bi, h: (bi, 0, k_off + h)),
            pl.BlockSpec((1, n, dv), lambda bi, h: (bi, 0, v_off + h)),
            pl.BlockSpec((1, n, dv), lambda bi, h: (bi, 0, g_off + h)),
            pl.BlockSpec((1, 8, dv), lambda bi, h: (h, 0, 0)),
        ],
        out_specs=pl.BlockSpec((1, n, dv), lambda bi, h: (bi, 0, h)),
        out_shape=jax.ShapeDtypeStruct((b, n, RET_HEADS * dv), BF16),
        scratch_shapes=[
            pltpu.VMEM((nc, dk, dv), BF16),
            pltpu.VMEM((dk, dv), F32),
            pltpu.VMEM((dk, dv), BF16),
            pltpu.VMEM((c, dk), F32),
            pltpu.VMEM((c, dk), F32),
            pltpu.VMEM((c, dk), F32),
            pltpu.VMEM((c, dk), F32),
            pltpu.VMEM((c, c), F32),
        ],
        compiler_params=_params("arbitrary", "arbitrary"),
        name="retention_core",
    )(proj, proj, proj, proj, dec)


def _out_mlp_kernel(x_ref, y_ref, mod_ref, wo_ref, g2_ref, w1_ref, w2_ref, fg_ref, o_ref,
                    acc_scr, *, final):
    m = mod_ref[0]
    x1 = x_ref[0] + m[2:3] * _dot(y_ref[0], wo_ref[...])
    o_ref[0] = x1
    h2 = _modulated_norm(x1, g2_ref[...], m[3:4], m[4:5])
    d_ff = w1_ref.shape[1]
    for cc in range(d_ff // FF_CHUNK):
        cs = slice(cc * FF_CHUNK, (cc + 1) * FF_CHUNK)
        u = jnp.maximum(_dot(h2, w1_ref[:, cs]), 0.0)
        part = _dot((u * u).astype(BF16), w2_ref[cs, :])
        if cc == 0:
            acc_scr[...] = part
        else:
            acc_scr[...] += part
    x2 = o_ref[0] + m[5:6] * acc_scr[...]
    if final:
        x2 = _rms(x2) * fg_ref[...]
    o_ref[0] = x2


def _out_mlp(x, y, mod, w_out, gain2, w1, w2, final_g, final):
    b, n, d = x.shape
    tm = MLP_TOK_TILE
    dy = y.shape[-1]
    return pl.pallas_call(
        functools.partial(_out_mlp_kernel, final=final),
        grid=(b, n // tm),
        in_specs=[
            pl.BlockSpec((1, tm, d), lambda bi, i: (bi, i, 0)),
            pl.BlockSpec((1, tm, dy), lambda bi, i: (bi, i, 0)),
            pl.BlockSpec((1, N_MOD, d), lambda bi, i: (bi, 0, 0)),
            _resident(w_out.shape),
            _resident(gain2.shape),
            _resident(w1.shape),
            _resident(w2.shape),
            _resident(final_g.shape),
        ],
        out_specs=pl.BlockSpec((1, tm, d), lambda bi, i: (bi, i, 0)),
        out_shape=jax.ShapeDtypeStruct((b, n, d), F32),
        scratch_shapes=[pltpu.VMEM((tm, d), F32)],
        compiler_params=_params("arbitrary", "arbitrary"),
        name="out_proj_mlp_final" if final else "out_proj_mlp",
    )(x, y, mod, w_out, gain2, w1, w2, final_g)


def _att_in_kernel(x_ref, mod_ref, g_ref, w_ref, qg_ref, kg_ref, cos_ref, sin_ref, o_ref, *, d):
    m = mod_ref[0]
    h = _modulated_norm(x_ref[0], g_ref[...], m[0:1], m[1:2])
    cos = cos_ref[...]
    sin = sin_ref[...]
    hd = ATT_HD
    nq = d // hd
    for hh in range(nq + ATT_KV_HEADS):
        cs = slice(hh * hd, (hh + 1) * hd)
        t = _rms(_dot(h, w_ref[:, cs]))
        if hh < nq:
            t = t * qg_ref[...]
        else:
            t = t * kg_ref[...]
        t = t * cos + pltpu.roll(t, hd // 2, 1) * sin
        if hh < nq:
            t = t * (hd ** -0.5)
        o_ref[0, :, cs] = t.astype(BF16)
    vs = slice((nq + ATT_KV_HEADS) * hd, (nq + 2 * ATT_KV_HEADS) * hd)
    o_ref[0, :, vs] = _dot(h, w_ref[:, vs]).astype(BF16)


def _att_in(x, mod, gain, w, q_gain, k_gain, cos, sin):
    b, n, d = x.shape
    tm = TOK_TILE
    nw = w.shape[1]
    return pl.pallas_call(
        functools.partial(_att_in_kernel, d=d),
        grid=(b, n // tm),
        in_specs=[
            pl.BlockSpec((1, tm, d), lambda bi, i: (bi, i, 0)),
            pl.BlockSpec((1, N_MOD, d), lambda bi, i: (bi, 0, 0)),
            _resident(gain.shape),
            _resident(w.shape),
            _resident(q_gain.shape),
            _resident(k_gain.shape),
            pl.BlockSpec((tm, ATT_HD), lambda bi, i: (i, 0)),
            pl.BlockSpec((tm, ATT_HD), lambda bi, i: (i, 0)),
        ],
        out_specs=pl.BlockSpec((1, tm, nw), lambda bi, i: (bi, i, 0)),
        out_shape=jax.ShapeDtypeStruct((b, n, nw), BF16),
        compiler_params=_params("arbitrary", "arbitrary"),
        name="att_in_proj",
    )(x, mod, gain, w, q_gain, k_gain, cos, sin)


def _attention_kernel(q_ref, k_ref, v_ref, o_ref, *, group):
    hd = ATT_HD
    tq = q_ref.shape[1]
    n = k_ref.shape[1]
    kb = ATT_K_BLOCK
    q = jnp.concatenate([q_ref[0, :, g * hd:(g + 1) * hd] for g in range(group)], axis=0)
    rows = group * tq

    def body(j, carry):
        m_i, l_i, acc = carry
        r = pl.ds(pl.multiple_of(j * kb, kb), kb)
        s = lax.dot_general(q, k_ref[0, r, :], (((1,), (1,)), ((), ())), preferred_element_type=F32)
        m_new = jnp.maximum(m_i, jnp.max(s, axis=-1, keepdims=True))
        alpha = jnp.exp(m_i - m_new)
        p = jnp.exp(s - m_new)
        l_new = alpha * l_i + jnp.sum(p, axis=-1, keepdims=True)
        acc_new = alpha * acc + _dot(p.astype(BF16), v_ref[0, r, :])
        return m_new, l_new, acc_new

    init = (jnp.full((rows, 1), -jnp.inf, F32), jnp.zeros((rows, 1), F32), jnp.zeros((rows, hd), F32))
    _, l_i, acc = lax.fori_loop(0, n // kb, body, init)
    out = acc / l_i
    for g in range(group):
        o_ref[0, :, g * hd:(g + 1) * hd] = out[g * tq:(g + 1) * tq].astype(BF16)


def _attention(proj, d):
    b, n, _ = proj.shape
    hd = ATT_HD
    nq = d // hd
    group = nq // ATT_KV_HEADS
    tq = ATT_Q_TILE
    return pl.pallas_call(
        functools.partial(_attention_kernel, group=group),
        grid=(b, ATT_KV_HEADS, n // tq),
        in_specs=[
            pl.BlockSpec((1, tq, group * hd), lambda bi, g, i: (bi, i, g)),
            pl.BlockSpec((1, n, hd), lambda bi, g, i: (bi, 0, nq + g)),
            pl.BlockSpec((1, n, hd), lambda bi, g, i: (bi, 0, nq + ATT_KV_HEADS + g)),
        ],
        out_specs=pl.BlockSpec((1, tq, group * hd), lambda bi, g, i: (bi, i, g)),
        out_shape=jax.ShapeDtypeStruct((b, n, d), BF16),
        compiler_params=_params("arbitrary", "arbitrary", "arbitrary"),
        name="gqa_attention",
    )(proj, proj, proj)


def _rope_angles(n, head_dim):
    rows = n // GRID_W
    row = jnp.repeat(jnp.arange(rows, dtype=F32), GRID_W)
    col = jnp.tile(jnp.arange(GRID_W, dtype=F32), rows)
    nf = head_dim // 4
    inv = ROPE_THETA ** (-jnp.arange(nf, dtype=F32) / nf)
    return row[:, None] * inv[None, :], col[:, None] * inv[None, :]


def _halves_first(head_dim):
    q = head_dim // 4
    return jnp.concatenate([jnp.arange(0, q), jnp.arange(2 * q, 3 * q),
                            jnp.arange(q, 2 * q), jnp.arange(3 * q, 4 * q)])


def _trunk(x, mod, p):
    b, n, d = x.shape
    proj = _ret_in(x, mod[0], p["norm1_g"][0], p["ret_w_in"], p["ret_cos"], p["ret_sin"])
    y = _retention(proj, p["ret_dec"], d)
    x = _out_mlp(x, y, mod[0], p["ret_w_out"], p["norm2_g"][0], p["mlp_w1"][0], p["mlp_w2"][0],
                 p["final_g"], final=False)
    proj = _att_in(x, mod[1], p["norm1_g"][1], p["att_w_in"], p["att_q_gain"], p["att_k_gain"],
                   p["att_cos"], p["att_sin"])
    y = _attention(proj, d)
    return _out_mlp(x, y, mod[1], p["att_w_out"], p["norm2_g"][1], p["mlp_w1"][1], p["mlp_w2"][1],
                    p["final_g"], final=True)


def kernel(x_prompt, x_sample, c_prompt, c_sample, mod_w, mod_b, norm1_g, norm2_g, ret_w_in, ret_decay,
           ret_w_out, att_w_in, att_q_gain, att_k_gain, att_w_out, mlp_w1, mlp_w2, final_g):
    depth, d, _ = mod_w.shape
    assert depth == 2 and ret_w_in.shape[0] == 1 and att_w_in.shape[0] == 1
    bp, n, _ = x_prompt.shape
    bs = x_sample.shape[0]
    assert x_sample.shape[1] == n

    c_all = jnp.concatenate([c_prompt, c_sample], axis=0)
    pad = (-c_all.shape[0]) % 16
    c_all = jnp.pad(c_all, ((0, pad), (0, 0)))
    mod = _adaln(c_all, mod_w, mod_b).reshape(depth, -1, N_MOD, d)

    dk = d // RET_HEADS
    perm_r = _halves_first(dk)
    qk_cols = (jnp.arange(2 * RET_HEADS)[:, None] * dk + perm_r[None, :]).reshape(-1)
    ret_cols = jnp.concatenate([qk_cols, jnp.arange(2 * d, ret_w_in.shape[2])])
    perm_a = _halves_first(ATT_HD)
    n_qk = d // ATT_HD + ATT_KV_HEADS
    att_qk_cols = (jnp.arange(n_qk)[:, None] * ATT_HD + perm_a[None, :]).reshape(-1)
    att_cols = jnp.concatenate([att_qk_cols, jnp.arange(n_qk * ATT_HD, att_w_in.shape[2])])

    ar, ac = _rope_angles(n, dk)
    br, bc = _rope_angles(n, ATT_HD)
    dv = 2 * d // RET_HEADS
    p = {
        "norm1_g": norm1_g.reshape(depth, 1, d),
        "norm2_g": norm2_g.reshape(depth, 1, d),
        "final_g": final_g.reshape(1, d),
        "ret_w_in": ret_w_in[0][:, ret_cols].astype(BF16),
        "ret_w_out": ret_w_out[0].astype(BF16),
        "att_w_in": att_w_in[0][:, att_cols].astype(BF16),
        "att_w_out": att_w_out[0].astype(BF16),
        "att_q_gain": att_q_gain[0][perm_a].reshape(1, ATT_HD),
        "att_k_gain": att_k_gain[0][perm_a].reshape(1, ATT_HD),
        "mlp_w1": mlp_w1.astype(BF16),
        "mlp_w2": mlp_w2.astype(BF16),
        "ret_cos": jnp.concatenate([jnp.cos(ar), jnp.cos(ac)], axis=-1),
        "ret_sin": jnp.concatenate([jnp.sin(ar), jnp.sin(ac)], axis=-1),
        "att_cos": jnp.concatenate([jnp.cos(br), jnp.cos(bc), jnp.cos(br), jnp.cos(bc)], axis=-1),
        "att_sin": jnp.concatenate([-jnp.sin(br), -jnp.sin(bc), jnp.sin(br), jnp.sin(bc)], axis=-1),
        "ret_dec": jnp.pad(jnp.broadcast_to(ret_decay[0].T[:, :, None], (RET_HEADS, 2, dv)),
                           ((0, 0), (0, 6), (0, 0))),
    }
    y_prompt = _trunk(x_prompt, mod[:, :bp], p)
    y_sample = _trunk(x_sample, mod[:, bp:bp + bs], p)
    return (y_prompt, y_sample)
```

```python
import functools

import jax
import jax.numpy as jnp
from jax import lax
from jax.experimental import pallas as pl
from jax.experimental.pallas import tpu as pltpu

F32 = jnp.float32
BF16 = jnp.bfloat16

GRID_W = 64
ROPE_THETA = 10000.0
EPS = 1e-6
RET_HEADS = 4
ATT_HD = 128
ATT_KV_HEADS = 2
N_MOD = 6

V7X_VMEM_BYTES = 64 * 1024 * 1024
V7X_LANES = 128
V7X_MXU_DIM = 256

VMEM_LIMIT = V7X_VMEM_BYTES - 8 * 1024 * 1024
RET_TOK_TILE = 512
ATT_TOK_TILE = 1024
MLP_TOK_TILE = 512
FF_CHUNK = 1024
RET_CHUNK = V7X_MXU_DIM
ATT_Q_TILE = 256
LOG2_E = 1.4426950408889634


def _params(*sem):
    return pltpu.CompilerParams(dimension_semantics=sem, vmem_limit_bytes=VMEM_LIMIT)


def _resident(shape):
    nd = len(shape)
    return pl.BlockSpec(shape, lambda *_: (0,) * nd, pipeline_mode=pl.Buffered(1))


def _rms(x):
    return x * lax.rsqrt(jnp.mean(x * x, axis=-1, keepdims=True) + EPS)


def _dot(a, b):
    return jnp.dot(a, b, preferred_element_type=F32)


def _adaln_kernel(c_ref, w_ref, b_ref, o_ref):
    c = c_ref[...]
    s = c * jax.nn.sigmoid(c)
    w = w_ref[0]
    s_hi = s.astype(BF16)
    s_lo = (s - s_hi.astype(F32)).astype(BF16)
    w_hi = w.astype(BF16)
    w_lo = (w - w_hi.astype(F32)).astype(BF16)
    o_ref[0] = _dot(s_hi, w_hi) + _dot(s_lo, w_hi) + _dot(s_hi, w_lo) + b_ref[0]


def _adaln(c_all, mod_w, mod_b):
    depth, d, n6 = mod_w.shape
    rows = c_all.shape[0]
    tn = n6 // 4
    return pl.pallas_call(
        _adaln_kernel,
        grid=(depth, n6 // tn),
        in_specs=[
            pl.BlockSpec((rows, d), lambda i, j: (0, 0)),
            pl.BlockSpec((1, d, tn), lambda i, j: (i, 0, j)),
            pl.BlockSpec((1, 1, tn), lambda i, j: (i, 0, j)),
        ],
        out_specs=pl.BlockSpec((1, rows, tn), lambda i, j: (i, 0, j)),
        out_shape=jax.ShapeDtypeStruct((depth, rows, n6), F32),
        compiler_params=_params("arbitrary", "arbitrary"),
        name="adaln_mod",
    )(c_all, mod_w, mod_b.reshape(depth, 1, n6))


def _modulated_norm(x, gain, shift, scale):
    return (_rms(x) * gain * (1.0 + scale) + shift).astype(BF16)


def _ret_in_kernel(x_ref, mod_ref, g_ref, w_ref, cos_ref, sin_ref,
                   q_ref, kt_ref, v_ref, gate_ref, h_scr, *, d):
    m = mod_ref[0]
    h_scr[...] = _modulated_norm(x_ref[0], g_ref[...], m[0:1], m[1:2])
    cos = cos_ref[...]
    sin = sin_ref[...]
    hw = V7X_MXU_DIM
    half = V7X_LANES
    c = RET_CHUNK
    tm = x_ref.shape[1]
    for hh in range(2 * RET_HEADS):
        acc = _dot(h_scr[...], w_ref[:, hh * hw:(hh + 1) * hw])
        if hh < RET_HEADS:
            acc = acc * (hw ** -0.5)
        a = acc[:, :half]
        b = acc[:, half:]
        ra = a * cos - b * sin
        rb = a * sin + b * cos
        if hh < RET_HEADS:
            q_ref[0, :, hh * hw:hh * hw + half] = ra.astype(BF16)
            q_ref[0, :, hh * hw + half:(hh + 1) * hw] = rb.astype(BF16)
        else:
            for ci in range(tm // c):
                kt_ref[0, hh - RET_HEADS, ci, :half, :] = ra[ci * c:(ci + 1) * c].T.astype(BF16)
                kt_ref[0, hh - RET_HEADS, ci, half:, :] = rb[ci * c:(ci + 1) * c].T.astype(BF16)
    v_off = 2 * d
    g_off = 4 * d
    for cc in range(2 * d // hw):
        cs = slice(cc * hw, (cc + 1) * hw)
        v_ref[0, :, cs] = _dot(h_scr[...], w_ref[:, v_off + cc * hw:v_off + (cc + 1) * hw]).astype(BF16)
    for cc in range(2 * d // hw):
        cs = slice(cc * hw, (cc + 1) * hw)
        g = _dot(h_scr[...], w_ref[:, g_off + cc * hw:g_off + (cc + 1) * hw])
        gate_ref[0, :, cs] = (g * jax.nn.sigmoid(g)).astype(BF16)


def _ret_in(x, mod, gain, w, cos, sin):
    b, n, d = x.shape
    tm = RET_TOK_TILE
    c = RET_CHUNK
    dk = d // RET_HEADS
    return pl.pallas_call(
        functools.partial(_ret_in_kernel, d=d),
        grid=(b, n // tm),
        in_specs=[
            pl.BlockSpec((1, tm, d), lambda bi, i: (bi, i, 0)),
            pl.BlockSpec((1, N_MOD, d), lambda bi, i: (bi, 0, 0)),
            _resident(gain.shape),
            _resident(w.shape),
            pl.BlockSpec((tm, V7X_LANES), lambda bi, i: (i, 0)),
            pl.BlockSpec((tm, V7X_LANES), lambda bi, i: (i, 0)),
        ],
        out_specs=[
            pl.BlockSpec((1, tm, d), lambda bi, i: (bi, i, 0)),
            pl.BlockSpec((1, RET_HEADS, tm // c, dk, c), lambda bi, i: (bi, 0, i, 0, 0)),
            pl.BlockSpec((1, tm, 2 * d), lambda bi, i: (bi, i, 0)),
            pl.BlockSpec((1, tm, 2 * d), lambda bi, i: (bi, i, 0)),
        ],
        out_shape=[
            jax.ShapeDtypeStruct((b, n, d), BF16),
            jax.ShapeDtypeStruct((b, RET_HEADS, n // c, dk, c), BF16),
            jax.ShapeDtypeStruct((b, n, 2 * d), BF16),
            jax.ShapeDtypeStruct((b, n, 2 * d), BF16),
        ],
        scratch_shapes=[pltpu.VMEM((tm, d), BF16)],
        compiler_params=_params("arbitrary", "arbitrary"),
        name="ret_in_proj",
    )(x, mod, gain, w, cos, sin)


def _retention_kernel(q_ref, kt_ref, v_ref, g_ref, dec_ref, o_ref,
                      sf_all, sb_all, sf_scr, sb_scr, qdf_scr, qdb_scr, kdf_scr, kdb_scr, dm_scr, *, nc):
    c = RET_CHUNK
    dk = q_ref.shape[-1]
    dv = v_ref.shape[-1]
    dec = dec_ref[0]
    neg = -dec
    log_g = -(jnp.maximum(neg, 0.0) + jnp.log1p(jnp.exp(-jnp.abs(neg))))
    lf = log_g[0:1, :]
    lb = log_g[1:2, :]
    qpos = lax.broadcasted_iota(jnp.int32, (c, dk), 0).astype(F32)
    qdf_scr[...] = jnp.exp(lf[:, :dk] * (qpos + 1.0))
    qdb_scr[...] = jnp.exp(lb[:, :dk] * (c - qpos))
    kpos = lax.broadcasted_iota(jnp.int32, (dk, c), 1).astype(F32)
    kdf_scr[...] = jnp.exp(lf[:, :c] * (c - 1.0 - kpos))
    kdb_scr[...] = jnp.exp(lb[:, :c] * kpos)
    diff = (lax.broadcasted_iota(jnp.int32, (c, c), 0)
            - lax.broadcasted_iota(jnp.int32, (c, c), 1)).astype(F32)
    dm_scr[...] = jnp.where(diff >= 0.0,
                            jnp.exp(lf[:, :c] * jnp.maximum(diff, 0.0)),
                            jnp.exp(lb[:, :c] * jnp.maximum(-diff, 0.0)))
    cdf = jnp.exp(lf[:, :dv] * c)
    cdb = jnp.exp(lb[:, :dv] * c)

    def rows(i):
        return pl.ds(pl.multiple_of(i * c, c), c)

    sf_scr[...] = jnp.zeros_like(sf_scr)
    sb_scr[...] = jnp.zeros_like(sb_scr)

    def scan_body(t, carry):
        i = t
        sf_all[i] = sf_scr[...].astype(BF16)
        kf = (kt_ref[0, 0, i].astype(F32) * kdf_scr[...]).astype(BF16)
        sf_scr[...] = sf_scr[...] * cdf + _dot(kf, v_ref[0, rows(i), :])
        j = nc - 1 - t
        sb_all[j] = sb_scr[...].astype(BF16)
        kb = (kt_ref[0, 0, j].astype(F32) * kdb_scr[...]).astype(BF16)
        sb_scr[...] = sb_scr[...] * cdb + _dot(kb, v_ref[0, rows(j), :])
        return carry

    lax.fori_loop(0, nc - 1, scan_body, 0)
    sf_all[nc - 1] = sf_scr[...].astype(BF16)
    sb_all[0] = sb_scr[...].astype(BF16)

    def out_body(i, carry):
        r = rows(i)
        q = q_ref[0, r, :]
        v = v_ref[0, r, :]
        p = (_dot(q, kt_ref[0, 0, i]) * dm_scr[...]).astype(BF16)
        q32 = q.astype(F32)
        qf = (q32 * qdf_scr[...]).astype(BF16)
        qb = (q32 * qdb_scr[...]).astype(BF16)
        y = _dot(p, v) + _dot(qf, sf_all[i]) + _dot(qb, sb_all[i])
        mu = jnp.mean(y, axis=-1, keepdims=True)
        yc = y - mu
        var = jnp.mean(yc * yc, axis=-1, keepdims=True)
        yn = yc * lax.rsqrt(var + EPS)
        o_ref[0, r, :] = (yn * g_ref[0, r, :].astype(F32)).astype(BF16)
        return carry

    lax.fori_loop(0, nc, out_body, 0, unroll=2)


def _retention(q, kt, v, gate, dec):
    b, n, d = q.shape
    dk = d // RET_HEADS
    dv = 2 * d // RET_HEADS
    c = RET_CHUNK
    nc = n // c
    return pl.pallas_call(
        functools.partial(_retention_kernel, nc=nc),
        grid=(b, RET_HEADS),
        in_specs=[
            pl.BlockSpec((1, n, dk), lambda bi, h: (bi, 0, h)),
            pl.BlockSpec((1, 1, nc, dk, c), lambda bi, h: (bi, h, 0, 0, 0)),
            pl.BlockSpec((1, n, dv), lambda bi, h: (bi, 0, h)),
            pl.BlockSpec((1, n, dv), lambda bi, h: (bi, 0, h)),
            pl.BlockSpec((1, 8, dv), lambda bi, h: (h, 0, 0)),
        ],
        out_specs=pl.BlockSpec((1, n, dv), lambda bi, h: (bi, 0, h)),
        out_shape=jax.ShapeDtypeStruct((b, n, RET_HEADS * dv), BF16),
        scratch_shapes=[
            pltpu.VMEM((nc, dk, dv), BF16),
            pltpu.VMEM((nc, dk, dv), BF16),
            pltpu.VMEM((dk, dv), F32),
            pltpu.VMEM((dk, dv), F32),
            pltpu.VMEM((c, dk), F32),
            pltpu.VMEM((c, dk), F32),
            pltpu.VMEM((dk, c), F32),
            pltpu.VMEM((dk, c), F32),
            pltpu.VMEM((c, c), F32),
        ],
        compiler_params=_params("arbitrary", "arbitrary"),
        name="retention_core",
    )(q, kt, v, gate, dec)


def _out_mlp_kernel(x_ref, y_ref, mod_ref, wo_ref, g2_ref, w1_ref, w2_ref, fg_ref, o_ref,
                    acc_scr, *, final):
    m = mod_ref[0]
    x1 = x_ref[0] + m[2:3] * _dot(y_ref[0], wo_ref[...])
    o_ref[0] = x1
    h2 = _modulated_norm(x1, g2_ref[...], m[3:4], m[4:5])
    d_ff = w1_ref.shape[1]
    for cc in range(d_ff // FF_CHUNK):
        cs = slice(cc * FF_CHUNK, (cc + 1) * FF_CHUNK)
        u = jnp.maximum(_dot(h2, w1_ref[:, cs]), 0.0)
        part = _dot((u * u).astype(BF16), w2_ref[cs, :])
        if cc == 0:
            acc_scr[...] = part
        else:
            acc_scr[...] += part
    x2 = o_ref[0] + m[5:6] * acc_scr[...]
    if final:
        x2 = _rms(x2) * fg_ref[...]
    o_ref[0] = x2


def _out_mlp(x, y, mod, w_out, gain2, w1, w2, final_g, final):
    b, n, d = x.shape
    tm = MLP_TOK_TILE
    dy = y.shape[-1]
    return pl.pallas_call(
        functools.partial(_out_mlp_kernel, final=final),
        grid=(b, n // tm),
        in_specs=[
            pl.BlockSpec((1, tm, d), lambda bi, i: (bi, i, 0)),
            pl.BlockSpec((1, tm, dy), lambda bi, i: (bi, i, 0)),
            pl.BlockSpec((1, N_MOD, d), lambda bi, i: (bi, 0, 0)),
            _resident(w_out.shape),
            _resident(gain2.shape),
            _resident(w1.shape),
            _resident(w2.shape),
            _resident(final_g.shape),
        ],
        out_specs=pl.BlockSpec((1, tm, d), lambda bi, i: (bi, i, 0)),
        out_shape=jax.ShapeDtypeStruct((b, n, d), F32),
        scratch_shapes=[pltpu.VMEM((tm, d), F32)],
        compiler_params=_params("arbitrary", "arbitrary"),
        name="out_proj_mlp_final" if final else "out_proj_mlp",
    )(x, y, mod, w_out, gain2, w1, w2, final_g)


def _att_in_kernel(x_ref, mod_ref, g_ref, w_ref, qg_ref, qgr_ref, kg_ref, kgr_ref, cos_ref, sin_ref,
                   q_ref, kt_ref, vx_ref, h_scr, *, d):
    m = mod_ref[0]
    h_scr[...] = _modulated_norm(x_ref[0], g_ref[...], m[0:1], m[1:2])
    hd = ATT_HD
    hw = V7X_MXU_DIM
    nq = d // hd
    q_scale = hd ** -0.5 * LOG2_E
    cos = cos_ref[...]
    sin = sin_ref[...]
    tab = {
        "q": (cos * (qg_ref[...] * q_scale), sin * (qgr_ref[...] * q_scale)),
        "k": (cos * kg_ref[...], sin * kgr_ref[...]),
    }
    for pair in range((nq + ATT_KV_HEADS) * hd // hw):
        acc = _dot(h_scr[...], w_ref[:, pair * hw:(pair + 1) * hw])
        for sub in range(hw // hd):
            hh = pair * (hw // hd) + sub
            t = acc[:, sub * hd:(sub + 1) * hd]
            ta, tb = tab["q"] if hh < nq else tab["k"]
            r = lax.rsqrt(jnp.mean(t * t, axis=-1, keepdims=True) + EPS)
            t = (t * ta + pltpu.roll(t, hd // 2, 1) * tb) * r
            if hh < nq:
                q_ref[0, :, hh * hd:(hh + 1) * hd] = t.astype(BF16)
            else:
                kt_ref[0, hh - nq] = t.T.astype(BF16)
    v_off = (nq + ATT_KV_HEADS) * hd
    v = _dot(h_scr[...], w_ref[:, v_off:v_off + ATT_KV_HEADS * hd])
    for g in range(ATT_KV_HEADS):
        vx_ref[0, :, 2 * g * hd:(2 * g + 1) * hd] = v[:, g * hd:(g + 1) * hd].astype(BF16)
        vx_ref[0, :, (2 * g + 1) * hd:(2 * g + 2) * hd] = jnp.ones((x_ref.shape[1], hd), BF16)


def _att_in(x, mod, gain, w, q_gain, q_gain_r, k_gain, k_gain_r, cos, sin):
    b, n, d = x.shape
    tm = ATT_TOK_TILE
    hd = ATT_HD
    return pl.pallas_call(
        functools.partial(_att_in_kernel, d=d),
        grid=(b, n // tm),
        in_specs=[
            pl.BlockSpec((1, tm, d), lambda bi, i: (bi, i, 0)),
            pl.BlockSpec((1, N_MOD, d), lambda bi, i: (bi, 0, 0)),
            _resident(gain.shape),
            _resident(w.shape),
            _resident(q_gain.shape),
            _resident(q_gain_r.shape),
            _resident(k_gain.shape),
            _resident(k_gain_r.shape),
            pl.BlockSpec((tm, hd), lambda bi, i: (i, 0)),
            pl.BlockSpec((tm, hd), lambda bi, i: (i, 0)),
        ],
        out_specs=[
            pl.BlockSpec((1, tm, d), lambda bi, i: (bi, i, 0)),
            pl.BlockSpec((1, ATT_KV_HEADS, hd, tm), lambda bi, i: (bi, 0, 0, i)),
            pl.BlockSpec((1, tm, 2 * ATT_KV_HEADS * hd), lambda bi, i: (bi, i, 0)),
        ],
        out_shape=[
            jax.ShapeDtypeStruct((b, n, d), BF16),
            jax.ShapeDtypeStruct((b, ATT_KV_HEADS, hd, n), BF16),
            jax.ShapeDtypeStruct((b, n, 2 * ATT_KV_HEADS * hd), BF16),
        ],
        scratch_shapes=[pltpu.VMEM((tm, d), BF16)],
        compiler_params=_params("arbitrary", "arbitrary"),
        name="att_in_proj",
    )(x, mod, gain, w, q_gain, q_gain_r, k_gain, k_gain_r, cos, sin)


def _attention_kernel(q_ref, kt_ref, vx_ref, o_ref, s_scr, p_scr, *, group, tq):
    hd = ATT_HD
    nt = q_ref.shape[1] // tq

    def rows(i):
        return pl.ds(pl.multiple_of(i * tq, tq), tq)

    def scores(i, g):
        s_scr[g % 2] = _dot(q_ref[0, rows(i), g * hd:(g + 1) * hd], kt_ref[0, 0])

    def probs(g):
        m = jnp.max(s_scr[g % 2], axis=-1, keepdims=True)
        p_scr[g % 2] = jnp.exp2(s_scr[g % 2] - m).astype(BF16)

    def weighted(i, g):
        acc = _dot(p_scr[g % 2], vx_ref[0])
        o_ref[0, rows(i), g * hd:(g + 1) * hd] = (acc[:, :hd] / acc[:, hd:]).astype(BF16)

    scores(0, 0)

    def body(i, carry):
        for g in range(group):
            probs(g)
            if g + 1 < group:
                scores(i, g + 1)
            else:
                scores(jnp.minimum(i + 1, nt - 1), 0)
            weighted(i, g)
        return carry

    lax.fori_loop(0, nt, body, 0)


def _attention(q, kt, vx):
    b, n, d = q.shape
    hd = ATT_HD
    group = d // hd // ATT_KV_HEADS
    tq = ATT_Q_TILE
    assert group % 2 == 0
    return pl.pallas_call(
        functools.partial(_attention_kernel, group=group, tq=tq),
        grid=(b, ATT_KV_HEADS),
        in_specs=[
            pl.BlockSpec((1, n, group * hd), lambda bi, g: (bi, 0, g)),
            pl.BlockSpec((1, 1, hd, n), lambda bi, g: (bi, g, 0, 0)),
            pl.BlockSpec((1, n, 2 * hd), lambda bi, g: (bi, 0, g)),
        ],
        out_specs=pl.BlockSpec((1, n, group * hd), lambda bi, g: (bi, 0, g)),
        out_shape=jax.ShapeDtypeStruct((b, n, d), BF16),
        scratch_shapes=[pltpu.VMEM((2, tq, n), F32), pltpu.VMEM((2, tq, n), BF16)],
        compiler_params=_params("arbitrary", "arbitrary"),
        name="gqa_attention",
    )(q, kt, vx)


def _rope_angles(n, head_dim):
    rows = n // GRID_W
    row = jnp.repeat(jnp.arange(rows, dtype=F32), GRID_W)
    col = jnp.tile(jnp.arange(GRID_W, dtype=F32), rows)
    nf = head_dim // 4
    inv = ROPE_THETA ** (-jnp.arange(nf, dtype=F32) / nf)
    return row[:, None] * inv[None, :], col[:, None] * inv[None, :]


def _halves_first(head_dim):
    q = head_dim // 4
    return jnp.concatenate([jnp.arange(0, q), jnp.arange(2 * q, 3 * q),
                            jnp.arange(q, 2 * q), jnp.arange(3 * q, 4 * q)])


def _trunk(x, mod, p):
    q, kt, v, gate = _ret_in(x, mod[0], p["norm1_g"][0], p["ret_w_in"], p["ret_cos"], p["ret_sin"])
    y = _retention(q, kt, v, gate, p["ret_dec"])
    x = _out_mlp(x, y, mod[0], p["ret_w_out"], p["norm2_g"][0], p["mlp_w1"][0], p["mlp_w2"][0],
                 p["final_g"], final=False)
    q, kt, vx = _att_in(x, mod[1], p["norm1_g"][1], p["att_w_in"], p["att_q_gain"], p["att_q_gain_r"],
                        p["att_k_gain"], p["att_k_gain_r"], p["att_cos"], p["att_sin"])
    y = _attention(q, kt, vx)
    return _out_mlp(x, y, mod[1], p["att_w_out"], p["norm2_g"][1], p["mlp_w1"][1], p["mlp_w2"][1],
                    p["final_g"], final=True)


def kernel(x_prompt, x_sample, c_prompt, c_sample, mod_w, mod_b, norm1_g, norm2_g, ret_w_in, ret_decay,
           ret_w_out, att_w_in, att_q_gain, att_k_gain, att_w_out, mlp_w1, mlp_w2, final_g):
    depth, d, _ = mod_w.shape
    assert depth == 2 and ret_w_in.shape[0] == 1 and att_w_in.shape[0] == 1
    bp, n, _ = x_prompt.shape
    bs = x_sample.shape[0]
    assert x_sample.shape[1] == n

    c_all = jnp.concatenate([c_prompt, c_sample], axis=0)
    pad = (-c_all.shape[0]) % 16
    c_all = jnp.pad(c_all, ((0, pad), (0, 0)))
    mod = _adaln(c_all, mod_w, mod_b).reshape(depth, -1, N_MOD, d)

    dk = d // RET_HEADS
    perm_r = _halves_first(dk)
    qk_cols = (jnp.arange(2 * RET_HEADS)[:, None] * dk + perm_r[None, :]).reshape(-1)
    ret_cols = jnp.concatenate([qk_cols, jnp.arange(2 * d, ret_w_in.shape[2])])
    perm_a = _halves_first(ATT_HD)
    n_qk = d // ATT_HD + ATT_KV_HEADS
    att_qk_cols = (jnp.arange(n_qk)[:, None] * ATT_HD + perm_a[None, :]).reshape(-1)
    att_cols = jnp.concatenate([att_qk_cols, jnp.arange(n_qk * ATT_HD, att_w_in.shape[2])])

    ar, ac = _rope_angles(n, dk)
    br, bc = _rope_angles(n, ATT_HD)
    dv = 2 * d // RET_HEADS
    q_gain = att_q_gain[0][perm_a]
    k_gain = att_k_gain[0][perm_a]
    p = {
        "norm1_g": norm1_g.reshape(depth, 1, d),
        "norm2_g": norm2_g.reshape(depth, 1, d),
        "final_g": final_g.reshape(1, d),
        "ret_w_in": ret_w_in[0][:, ret_cols].astype(BF16),
        "ret_w_out": ret_w_out[0].astype(BF16),
        "att_w_in": att_w_in[0][:, att_cols].astype(BF16),
        "att_w_out": att_w_out[0].astype(BF16),
        "att_q_gain": q_gain.reshape(1, ATT_HD),
        "att_q_gain_r": jnp.roll(q_gain, ATT_HD // 2).reshape(1, ATT_HD),
        "att_k_gain": k_gain.reshape(1, ATT_HD),
        "att_k_gain_r": jnp.roll(k_gain, ATT_HD // 2).reshape(1, ATT_HD),
        "mlp_w1": mlp_w1.astype(BF16),
        "mlp_w2": mlp_w2.astype(BF16),
        "ret_cos": jnp.concatenate([jnp.cos(ar), jnp.cos(ac)], axis=-1),
        "ret_sin": jnp.concatenate([jnp.sin(ar), jnp.sin(ac)], axis=-1),
        "att_cos": jnp.concatenate([jnp.cos(br), jnp.cos(bc), jnp.cos(br), jnp.cos(bc)], axis=-1),
        "att_sin": jnp.concatenate([-jnp.sin(br), -jnp.sin(bc), jnp.sin(br), jnp.sin(bc)], axis=-1),
        "ret_dec": jnp.pad(jnp.broadcast_to(ret_decay[0].T[:, :, None], (RET_HEADS, 2, dv)),
                           ((0, 0), (0, 6), (0, 0))),
    }
    y_prompt = _trunk(x_prompt, mod[:, :bp], p)
    y_sample = _trunk(x_sample, mod[:, bp:bp + bs], p)
    return (y_prompt, y_sample)
```

```python
import functools

import jax
import jax.numpy as jnp
from jax import lax
from jax.experimental import pallas as pl
from jax.experimental.pallas import tpu as pltpu

F32 = jnp.float32
BF16 = jnp.bfloat16

GRID_W = 64
ROPE_THETA = 10000.0
EPS = 1e-6
RET_HEADS = 4
ATT_HD = 128
ATT_KV_HEADS = 2
N_MOD = 6

V7X_VMEM_BYTES = 64 * 1024 * 1024
V7X_LANES = 128
V7X_MXU_DIM = 256

VMEM_LIMIT = V7X_VMEM_BYTES - 8 * 1024 * 1024
RET_TOK_TILE = 512
ATT_TOK_TILE = 1024
MLP_TOK_TILE = 512
FF_CHUNK = 1024
RET_CHUNK = V7X_MXU_DIM
SCAN_UNROLL = 5
OUT_UNROLL = 8
ATT_Q_TILE = 512
LOG2_E = 1.4426950408889634


def _params(*sem):
    return pltpu.CompilerParams(dimension_semantics=sem, vmem_limit_bytes=VMEM_LIMIT)


def _resident(shape):
    nd = len(shape)
    return pl.BlockSpec(shape, lambda *_: (0,) * nd, pipeline_mode=pl.Buffered(1))


def _rms(x):
    return x * lax.rsqrt(jnp.mean(x * x, axis=-1, keepdims=True) + EPS)


def _dot(a, b):
    return jnp.dot(a, b, preferred_element_type=F32)


def _adaln_kernel(c_ref, w_ref, b_ref, o_ref):
    c = c_ref[...]
    s = c * jax.nn.sigmoid(c)
    w = w_ref[0]
    s_hi = s.astype(BF16)
    s_lo = (s - s_hi.astype(F32)).astype(BF16)
    w_hi = w.astype(BF16)
    w_lo = (w - w_hi.astype(F32)).astype(BF16)
    o_ref[0] = _dot(s_hi, w_hi) + _dot(s_lo, w_hi) + _dot(s_hi, w_lo) + b_ref[0]


def _adaln(c_all, mod_w, mod_b):
    depth, d, n6 = mod_w.shape
    rows = c_all.shape[0]
    tn = n6 // 4
    return pl.pallas_call(
        _adaln_kernel,
        grid=(depth, n6 // tn),
        in_specs=[
            pl.BlockSpec((rows, d), lambda i, j: (0, 0)),
            pl.BlockSpec((1, d, tn), lambda i, j: (i, 0, j)),
            pl.BlockSpec((1, 1, tn), lambda i, j: (i, 0, j)),
        ],
        out_specs=pl.BlockSpec((1, rows, tn), lambda i, j: (i, 0, j)),
        out_shape=jax.ShapeDtypeStruct((depth, rows, n6), F32),
        compiler_params=_params("arbitrary", "arbitrary"),
        name="adaln_mod",
    )(c_all, mod_w, mod_b.reshape(depth, 1, n6))


def _modulated_norm(x, gain, shift, scale):
    return (_rms(x) * gain * (1.0 + scale) + shift).astype(BF16)


def _ret_in_kernel(x_ref, mod_ref, g_ref, w_ref, cos_ref, sin_ref,
                   q_ref, kt_ref, v_ref, gate_ref, h_scr, *, d):
    m = mod_ref[0]
    h_scr[...] = _modulated_norm(x_ref[0], g_ref[...], m[0:1], m[1:2])
    cos = cos_ref[...]
    sin = sin_ref[...]
    hw = V7X_MXU_DIM
    half = V7X_LANES
    c = RET_CHUNK
    tm = x_ref.shape[1]
    for hh in range(2 * RET_HEADS):
        acc = _dot(h_scr[...], w_ref[:, hh * hw:(hh + 1) * hw])
        if hh < RET_HEADS:
            acc = acc * (hw ** -0.5)
        a = acc[:, :half]
        b = acc[:, half:]
        ra = a * cos - b * sin
        rb = a * sin + b * cos
        if hh < RET_HEADS:
            q_ref[0, :, hh * hw:hh * hw + half] = ra.astype(BF16)
            q_ref[0, :, hh * hw + half:(hh + 1) * hw] = rb.astype(BF16)
        else:
            for ci in range(tm // c):
                kt_ref[0, hh - RET_HEADS, ci, :half, :] = ra[ci * c:(ci + 1) * c].T.astype(BF16)
                kt_ref[0, hh - RET_HEADS, ci, half:, :] = rb[ci * c:(ci + 1) * c].T.astype(BF16)
    v_off = 2 * d
    g_off = 4 * d
    for cc in range(2 * d // hw):
        cs = slice(cc * hw, (cc + 1) * hw)
        v_ref[0, :, cs] = _dot(h_scr[...], w_ref[:, v_off + cc * hw:v_off + (cc + 1) * hw]).astype(BF16)
    for cc in range(2 * d // hw):
        cs = slice(cc * hw, (cc + 1) * hw)
        g = _dot(h_scr[...], w_ref[:, g_off + cc * hw:g_off + (cc + 1) * hw])
        gate_ref[0, :, cs] = (g * jax.nn.sigmoid(g)).astype(BF16)


def _ret_in(x, mod, gain, w, cos, sin):
    b, n, d = x.shape
    tm = RET_TOK_TILE
    c = RET_CHUNK
    dk = d // RET_HEADS
    return pl.pallas_call(
        functools.partial(_ret_in_kernel, d=d),
        grid=(b, n // tm),
        in_specs=[
            pl.BlockSpec((1, tm, d), lambda bi, i: (bi, i, 0)),
            pl.BlockSpec((1, N_MOD, d), lambda bi, i: (bi, 0, 0)),
            _resident(gain.shape),
            _resident(w.shape),
            pl.BlockSpec((tm, V7X_LANES), lambda bi, i: (i, 0)),
            pl.BlockSpec((tm, V7X_LANES), lambda bi, i: (i, 0)),
        ],
        out_specs=[
            pl.BlockSpec((1, tm, d), lambda bi, i: (bi, i, 0)),
            pl.BlockSpec((1, RET_HEADS, tm // c, dk, c), lambda bi, i: (bi, 0, i, 0, 0)),
            pl.BlockSpec((1, tm, 2 * d), lambda bi, i: (bi, i, 0)),
            pl.BlockSpec((1, tm, 2 * d), lambda bi, i: (bi, i, 0)),
        ],
        out_shape=[
            jax.ShapeDtypeStruct((b, n, d), BF16),
            jax.ShapeDtypeStruct((b, RET_HEADS, n // c, dk, c), BF16),
            jax.ShapeDtypeStruct((b, n, 2 * d), BF16),
            jax.ShapeDtypeStruct((b, n, 2 * d), BF16),
        ],
        scratch_shapes=[pltpu.VMEM((tm, d), BF16)],
        compiler_params=_params("arbitrary", "arbitrary"),
        name="ret_in_proj",
    )(x, mod, gain, w, cos, sin)


def _retention_kernel(q_ref, kt_ref, v_ref, g_ref, dec_ref, o_ref,
                      sf_all, sb_all, sf_scr, sb_scr, qdf_scr, qdb_scr, kdf_scr, kdb_scr, dm_scr, *, nc):
    c = RET_CHUNK
    dk = q_ref.shape[-1]
    dv = v_ref.shape[-1]
    dec = dec_ref[0]
    neg = -dec
    log_g = -(jnp.maximum(neg, 0.0) + jnp.log1p(jnp.exp(-jnp.abs(neg))))
    lf = log_g[0:1, :]
    lb = log_g[1:2, :]
    qpos = lax.broadcasted_iota(jnp.int32, (c, dk), 0).astype(F32)
    qdf_scr[...] = jnp.exp(lf[:, :dk] * (qpos + 1.0))
    qdb_scr[...] = jnp.exp(lb[:, :dk] * (c - qpos))
    kpos = lax.broadcasted_iota(jnp.int32, (dk, c), 1).astype(F32)
    kdf_scr[...] = jnp.exp(lf[:, :c] * (c - 1.0 - kpos))
    kdb_scr[...] = jnp.exp(lb[:, :c] * kpos)
    diff = (lax.broadcasted_iota(jnp.int32, (c, c), 0)
            - lax.broadcasted_iota(jnp.int32, (c, c), 1)).astype(F32)
    dm_scr[...] = jnp.where(diff >= 0.0,
                            jnp.exp(lf[:, :c] * jnp.maximum(diff, 0.0)),
                            jnp.exp(lb[:, :c] * jnp.maximum(-diff, 0.0)))
    cdf = jnp.exp(lf[:, :dv] * c)
    cdb = jnp.exp(lb[:, :dv] * c)

    def rows(i):
        return pl.ds(pl.multiple_of(i * c, c), c)

    sf_scr[...] = jnp.zeros_like(sf_scr)
    sb_scr[...] = jnp.zeros_like(sb_scr)
    sf_all[0] = jnp.zeros((dk, dv), BF16)
    sb_all[nc - 1] = jnp.zeros((dk, dv), BF16)

    def scan_body(t, carry):
        kf = (kt_ref[0, 0, t].astype(F32) * kdf_scr[...]).astype(BF16)
        sf = sf_scr[...] * cdf + _dot(kf, v_ref[0, rows(t), :])
        sf_scr[...] = sf
        sf_all[t + 1] = sf.astype(BF16)
        j = nc - 1 - t
        kb = (kt_ref[0, 0, j].astype(F32) * kdb_scr[...]).astype(BF16)
        sb = sb_scr[...] * cdb + _dot(kb, v_ref[0, rows(j), :])
        sb_scr[...] = sb
        sb_all[j - 1] = sb.astype(BF16)
        return carry

    lax.fori_loop(0, nc - 1, scan_body, 0, unroll=SCAN_UNROLL)

    def out_body(i, carry):
        r = rows(i)
        q = q_ref[0, r, :]
        v = v_ref[0, r, :]
        p = (_dot(q, kt_ref[0, 0, i]) * dm_scr[...]).astype(BF16)
        q32 = q.astype(F32)
        qf = (q32 * qdf_scr[...]).astype(BF16)
        qb = (q32 * qdb_scr[...]).astype(BF16)
        y = _dot(p, v) + _dot(qf, sf_all[i]) + _dot(qb, sb_all[i])
        mu = jnp.mean(y, axis=-1, keepdims=True)
        yc = y - mu
        var = jnp.mean(yc * yc, axis=-1, keepdims=True)
        yn = yc * lax.rsqrt(var + EPS)
        o_ref[0, r, :] = (yn * g_ref[0, r, :].astype(F32)).astype(BF16)
        return carry

    lax.fori_loop(0, nc, out_body, 0, unroll=OUT_UNROLL)


def _retention(q, kt, v, gate, dec):
    b, n, d = q.shape
    dk = d // RET_HEADS
    dv = 2 * d // RET_HEADS
    c = RET_CHUNK
    nc = n // c
    return pl.pallas_call(
        functools.partial(_retention_kernel, nc=nc),
        grid=(b, RET_HEADS),
        in_specs=[
            pl.BlockSpec((1, n, dk), lambda bi, h: (bi, 0, h)),
            pl.BlockSpec((1, 1, nc, dk, c), lambda bi, h: (bi, h, 0, 0, 0)),
            pl.BlockSpec((1, n, dv), lambda bi, h: (bi, 0, h)),
            pl.BlockSpec((1, n, dv), lambda bi, h: (bi, 0, h)),
            pl.BlockSpec((1, 8, dv), lambda bi, h: (h, 0, 0)),
        ],
        out_specs=pl.BlockSpec((1, n, dv), lambda bi, h: (bi, 0, h)),
        out_shape=jax.ShapeDtypeStruct((b, n, RET_HEADS * dv), BF16),
        scratch_shapes=[
            pltpu.VMEM((nc, dk, dv), BF16),
            pltpu.VMEM((nc, dk, dv), BF16),
            pltpu.VMEM((dk, dv), F32),
            pltpu.VMEM((dk, dv), F32),
            pltpu.VMEM((c, dk), F32),
            pltpu.VMEM((c, dk), F32),
            pltpu.VMEM((dk, c), F32),
            pltpu.VMEM((dk, c), F32),
            pltpu.VMEM((c, c), F32),
        ],
        compiler_params=_params("arbitrary", "arbitrary"),
        name="retention_core",
    )(q, kt, v, gate, dec)


def _out_mlp_kernel(x_ref, y_ref, mod_ref, wo_ref, g2_ref, w1_ref, w2_ref, fg_ref, o_ref,
                    acc_scr, *, final):
    m = mod_ref[0]
    x1 = x_ref[0] + m[2:3] * _dot(y_ref[0], wo_ref[...])
    o_ref[0] = x1
    h2 = _modulated_norm(x1, g2_ref[...], m[3:4], m[4:5])
    d_ff = w1_ref.shape[1]
    for cc in range(d_ff // FF_CHUNK):
        cs = slice(cc * FF_CHUNK, (cc + 1) * FF_CHUNK)
        u = jnp.maximum(_dot(h2, w1_ref[:, cs]), 0.0)
        part = _dot((u * u).astype(BF16), w2_ref[cs, :])
        if cc == 0:
            acc_scr[...] = part
        else:
            acc_scr[...] += part
    x2 = o_ref[0] + m[5:6] * acc_scr[...]
    if final:
        x2 = _rms(x2) * fg_ref[...]
    o_ref[0] = x2


def _out_mlp(x, y, mod, w_out, gain2, w1, w2, final_g, final):
    b, n, d = x.shape
    tm = MLP_TOK_TILE
    dy = y.shape[-1]
    return pl.pallas_call(
        functools.partial(_out_mlp_kernel, final=final),
        grid=(b, n // tm),
        in_specs=[
            pl.BlockSpec((1, tm, d), lambda bi, i: (bi, i, 0)),
            pl.BlockSpec((1, tm, dy), lambda bi, i: (bi, i, 0)),
            pl.BlockSpec((1, N_MOD, d), lambda bi, i: (bi, 0, 0)),
            _resident(w_out.shape),
            _resident(gain2.shape),
            _resident(w1.shape),
            _resident(w2.shape),
            _resident(final_g.shape),
        ],
        out_specs=pl.BlockSpec((1, tm, d), lambda bi, i: (bi, i, 0)),
        out_shape=jax.ShapeDtypeStruct((b, n, d), F32),
        scratch_shapes=[pltpu.VMEM((tm, d), F32)],
        compiler_params=_params("arbitrary", "arbitrary"),
        name="out_proj_mlp_final" if final else "out_proj_mlp",
    )(x, y, mod, w_out, gain2, w1, w2, final_g)


def _att_in_kernel(x_ref, mod_ref, g_ref, w_ref, qga_ref, qgb_ref, kga_ref, kgb_ref, cos_ref, sin_ref,
                   q_ref, kt_ref, vx_ref, h_scr, *, d):
    m = mod_ref[0]
    h_scr[...] = _modulated_norm(x_ref[0], g_ref[...], m[0:1], m[1:2])
    hd = ATT_HD
    hw = V7X_MXU_DIM
    half = hw // 2
    part = hd // 2
    n_q_slabs = d // hw
    cos = cos_ref[...]
    sin = sin_ref[...]

    def tables(ga, gb):
        return cos * ga, sin * gb, sin * ga, cos * gb

    q_scale = hd ** -0.5 * LOG2_E
    q_tab = tables(qga_ref[...] * q_scale, qgb_ref[...] * q_scale)
    k_tab = tables(kga_ref[...], kgb_ref[...])
    first = lax.broadcasted_iota(jnp.int32, (1, half), 1) < part
    for sl in range(n_q_slabs + 1):
        acc = _dot(h_scr[...], w_ref[:, sl * hw:(sl + 1) * hw])
        x = acc[:, :half]
        y = acc[:, half:]
        u = x * x + y * y
        ss0 = jnp.sum(jnp.where(first, u, 0.0), axis=-1, keepdims=True)
        ss1 = jnp.sum(jnp.where(first, 0.0, u), axis=-1, keepdims=True)
        r = lax.rsqrt(jnp.where(first, ss0, ss1) * (1.0 / hd) + EPS)
        c_a, s_b, s_a, c_b = q_tab if sl < n_q_slabs else k_tab
        ox = (x * c_a - y * s_b) * r
        oy = (x * s_a + y * c_b) * r
        if sl < n_q_slabs:
            q_ref[0, :, sl * hw:sl * hw + half] = ox.astype(BF16)
            q_ref[0, :, sl * hw + half:(sl + 1) * hw] = oy.astype(BF16)
        else:
            xt = ox.T.astype(BF16)
            yt = oy.T.astype(BF16)
            for g in range(ATT_KV_HEADS):
                a = xt[g * part:(g + 1) * part]
                b = yt[g * part:(g + 1) * part]
                kt_ref[0, g, 0 * part:1 * part] = a
                kt_ref[0, g, 1 * part:2 * part] = a
                kt_ref[0, g, 2 * part:3 * part] = b
                kt_ref[0, g, 3 * part:4 * part] = b
    v_off = (n_q_slabs + 1) * hw
    v = _dot(h_scr[...], w_ref[:, v_off:v_off + ATT_KV_HEADS * hd])
    for g in range(ATT_KV_HEADS):
        vx_ref[0, :, 2 * g * hd:(2 * g + 1) * hd] = v[:, g * hd:(g + 1) * hd].astype(BF16)
        vx_ref[0, :, (2 * g + 1) * hd:(2 * g + 2) * hd] = jnp.ones((x_ref.shape[1], hd), BF16)


def _att_in(x, mod, gain, w, q_gain_a, q_gain_b, k_gain_a, k_gain_b, cos, sin):
    b, n, d = x.shape
    tm = ATT_TOK_TILE
    hd = ATT_HD
    assert ATT_KV_HEADS * hd == V7X_MXU_DIM
    return pl.pallas_call(
        functools.partial(_att_in_kernel, d=d),
        grid=(b, n // tm),
        in_specs=[
            pl.BlockSpec((1, tm, d), lambda bi, i: (bi, i, 0)),
            pl.BlockSpec((1, N_MOD, d), lambda bi, i: (bi, 0, 0)),
            _resident(gain.shape),
            _resident(w.shape),
            _resident(q_gain_a.shape),
            _resident(q_gain_b.shape),
            _resident(k_gain_a.shape),
            _resident(k_gain_b.shape),
            pl.BlockSpec((tm, hd), lambda bi, i: (i, 0)),
            pl.BlockSpec((tm, hd), lambda bi, i: (i, 0)),
        ],
        out_specs=[
            pl.BlockSpec((1, tm, d), lambda bi, i: (bi, i, 0)),
            pl.BlockSpec((1, ATT_KV_HEADS, 2 * hd, tm), lambda bi, i: (bi, 0, 0, i)),
            pl.BlockSpec((1, tm, 2 * ATT_KV_HEADS * hd), lambda bi, i: (bi, i, 0)),
        ],
        out_shape=[
            jax.ShapeDtypeStruct((b, n, d), BF16),
            jax.ShapeDtypeStruct((b, ATT_KV_HEADS, 2 * hd, n), BF16),
            jax.ShapeDtypeStruct((b, n, 2 * ATT_KV_HEADS * hd), BF16),
        ],
        scratch_shapes=[pltpu.VMEM((tm, d), BF16)],
        compiler_params=_params("arbitrary", "arbitrary"),
        name="att_in_proj",
    )(x, mod, gain, w, q_gain_a, q_gain_b, k_gain_a, k_gain_b, cos, sin)


def _attention_kernel(q_ref, kt_ref, vx_ref, o_ref, s_scr, p_scr, *, group, tq):
    hd = ATT_HD
    nt = q_ref.shape[1] // tq

    def rows(i):
        return pl.ds(pl.multiple_of(i * tq, tq), tq)

    hw = V7X_MXU_DIM
    heads_per_slab = hw // hd
    lane_head = (lax.broadcasted_iota(jnp.int32, (1, hw), 1) // (hd // 2)) % heads_per_slab

    def scores(i, g):
        slab = q_ref[0, rows(i), (g // heads_per_slab) * hw:(g // heads_per_slab + 1) * hw]
        mine = lane_head == g % heads_per_slab
        s_scr[g % 2] = _dot(jnp.where(mine, slab, jnp.zeros_like(slab)), kt_ref[0, 0])

    def probs(g):
        m = jnp.max(s_scr[g % 2], axis=-1, keepdims=True)
        p_scr[g % 2] = jnp.exp2(s_scr[g % 2] - m).astype(BF16)

    def weighted(i, g):
        acc = _dot(p_scr[g % 2], vx_ref[0])
        o_ref[0, rows(i), g * hd:(g + 1) * hd] = (acc[:, :hd] / acc[:, hd:]).astype(BF16)

    scores(0, 0)

    def body(i, carry):
        for g in range(group):
            probs(g)
            if g + 1 < group:
                scores(i, g + 1)
            else:
                scores(jnp.minimum(i + 1, nt - 1), 0)
            weighted(i, g)
        return carry

    lax.fori_loop(0, nt, body, 0)


def _attention(q, kt, vx):
    b, n, d = q.shape
    hd = ATT_HD
    group = d // hd // ATT_KV_HEADS
    tq = ATT_Q_TILE
    assert group % 2 == 0
    return pl.pallas_call(
        functools.partial(_attention_kernel, group=group, tq=tq),
        grid=(b, ATT_KV_HEADS),
        in_specs=[
            pl.BlockSpec((1, n, group * hd), lambda bi, g: (bi, 0, g)),
            pl.BlockSpec((1, 1, 2 * hd, n), lambda bi, g: (bi, g, 0, 0)),
            pl.BlockSpec((1, n, 2 * hd), lambda bi, g: (bi, 0, g)),
        ],
        out_specs=pl.BlockSpec((1, n, group * hd), lambda bi, g: (bi, 0, g)),
        out_shape=jax.ShapeDtypeStruct((b, n, d), BF16),
        scratch_shapes=[pltpu.VMEM((2, tq, n), F32), pltpu.VMEM((2, tq, n), BF16)],
        compiler_params=_params("arbitrary", "arbitrary"),
        name="gqa_attention",
    )(q, kt, vx)


def _rope_angles(n, head_dim):
    rows = n // GRID_W
    row = jnp.repeat(jnp.arange(rows, dtype=F32), GRID_W)
    col = jnp.tile(jnp.arange(GRID_W, dtype=F32), rows)
    nf = head_dim // 4
    inv = ROPE_THETA ** (-jnp.arange(nf, dtype=F32) / nf)
    return row[:, None] * inv[None, :], col[:, None] * inv[None, :]


def _halves_first(head_dim):
    q = head_dim // 4
    return jnp.concatenate([jnp.arange(0, q), jnp.arange(2 * q, 3 * q),
                            jnp.arange(q, 2 * q), jnp.arange(3 * q, 4 * q)])


def _trunk(x, mod, p):
    q, kt, v, gate = _ret_in(x, mod[0], p["norm1_g"][0], p["ret_w_in"], p["ret_cos"], p["ret_sin"])
    y = _retention(q, kt, v, gate, p["ret_dec"])
    x = _out_mlp(x, y, mod[0], p["ret_w_out"], p["norm2_g"][0], p["mlp_w1"][0], p["mlp_w2"][0],
                 p["final_g"], final=False)
    q, kt, vx = _att_in(x, mod[1], p["norm1_g"][1], p["att_w_in"], p["att_q_gain_a"], p["att_q_gain_b"],
                        p["att_k_gain_a"], p["att_k_gain_b"], p["att_cos"], p["att_sin"])
    y = _attention(q, kt, vx)
    return _out_mlp(x, y, mod[1], p["att_w_out"], p["norm2_g"][1], p["mlp_w1"][1], p["mlp_w2"][1],
                    p["final_g"], final=True)


def kernel(x_prompt, x_sample, c_prompt, c_sample, mod_w, mod_b, norm1_g, norm2_g, ret_w_in, ret_decay,
           ret_w_out, att_w_in, att_q_gain, att_k_gain, att_w_out, mlp_w1, mlp_w2, final_g):
    depth, d, _ = mod_w.shape
    assert depth == 2 and ret_w_in.shape[0] == 1 and att_w_in.shape[0] == 1
    bp, n, _ = x_prompt.shape
    bs = x_sample.shape[0]
    assert x_sample.shape[1] == n

    c_all = jnp.concatenate([c_prompt, c_sample], axis=0)
    pad = (-c_all.shape[0]) % 16
    c_all = jnp.pad(c_all, ((0, pad), (0, 0)))
    mod = _adaln(c_all, mod_w, mod_b).reshape(depth, -1, N_MOD, d)

    dk = d // RET_HEADS
    perm_r = _halves_first(dk)
    qk_cols = (jnp.arange(2 * RET_HEADS)[:, None] * dk + perm_r[None, :]).reshape(-1)
    ret_cols = jnp.concatenate([qk_cols, jnp.arange(2 * d, ret_w_in.shape[2])])
    part_idx = _halves_first(ATT_HD).reshape(2, ATT_HD // 2)
    n_qk = d // ATT_HD + ATT_KV_HEADS
    slab_heads = jnp.arange(n_qk).reshape(-1, 2)
    att_qk_cols = (slab_heads[:, None, :, None] * ATT_HD + part_idx[None, :, None, :]).reshape(-1)
    att_cols = jnp.concatenate([att_qk_cols, jnp.arange(n_qk * ATT_HD, att_w_in.shape[2])])

    ar, ac = _rope_angles(n, dk)
    br, bc = _rope_angles(n, ATT_HD)
    dv = 2 * d // RET_HEADS

    def slab_gain(gain, part):
        return jnp.tile(gain[part_idx[part]], 2).reshape(1, ATT_HD)
    p = {
        "norm1_g": norm1_g.reshape(depth, 1, d),
        "norm2_g": norm2_g.reshape(depth, 1, d),
        "final_g": final_g.reshape(1, d),
        "ret_w_in": ret_w_in[0][:, ret_cols].astype(BF16),
        "ret_w_out": ret_w_out[0].astype(BF16),
        "att_w_in": att_w_in[0][:, att_cols].astype(BF16),
        "att_w_out": att_w_out[0].astype(BF16),
        "att_q_gain_a": slab_gain(att_q_gain[0], 0),
        "att_q_gain_b": slab_gain(att_q_gain[0], 1),
        "att_k_gain_a": slab_gain(att_k_gain[0], 0),
        "att_k_gain_b": slab_gain(att_k_gain[0], 1),
        "mlp_w1": mlp_w1.astype(BF16),
        "mlp_w2": mlp_w2.astype(BF16),
        "ret_cos": jnp.concatenate([jnp.cos(ar), jnp.cos(ac)], axis=-1),
        "ret_sin": jnp.concatenate([jnp.sin(ar), jnp.sin(ac)], axis=-1),
        "att_cos": jnp.concatenate([jnp.cos(br), jnp.cos(bc), jnp.cos(br), jnp.cos(bc)], axis=-1),
        "att_sin": jnp.concatenate([jnp.sin(br), jnp.sin(bc), jnp.sin(br), jnp.sin(bc)], axis=-1),
        "ret_dec": jnp.pad(jnp.broadcast_to(ret_decay[0].T[:, :, None], (RET_HEADS, 2, dv)),
                           ((0, 0), (0, 6), (0, 0))),
    }
    y_prompt = _trunk(x_prompt, mod[:, :bp], p)
    y_sample = _trunk(x_sample, mod[:, bp:bp + bs], p)
    return (y_prompt, y_sample)
```

```python
import functools

import jax
import jax.numpy as jnp
from jax import lax
from jax.experimental import pallas as pl
from jax.experimental.pallas import tpu as pltpu

F32 = jnp.float32
BF16 = jnp.bfloat16

GRID_W = 64
ROPE_THETA = 10000.0
EPS = 1e-6
RET_HEADS = 4
ATT_HD = 128
ATT_KV_HEADS = 2
N_MOD = 6

V7X_VMEM_BYTES = 64 * 1024 * 1024
V7X_LANES = 128
V7X_MXU_DIM = 256

VMEM_LIMIT = V7X_VMEM_BYTES - 8 * 1024 * 1024
RET_TOK_TILE = 1024
SUB_TILE = 512
ATT_TOK_TILE = 1024
MLP_TOK_TILE = 1024
FF_CHUNK = 1024
RET_CHUNK = V7X_MXU_DIM
SCAN_UNROLL = 5
OUT_UNROLL = 8
ATT_Q_TILE = 512
ATT_TILE_UNROLL = 2
LOG2_E = 1.4426950408889634


def _params(*sem):
    return pltpu.CompilerParams(dimension_semantics=sem, vmem_limit_bytes=VMEM_LIMIT)


def _resident(shape):
    nd = len(shape)
    return pl.BlockSpec(shape, lambda *_: (0,) * nd, pipeline_mode=pl.Buffered(1))


def _rms(x):
    return x * lax.rsqrt(jnp.mean(x * x, axis=-1, keepdims=True) + EPS)


def _dot(a, b):
    return jnp.dot(a, b, preferred_element_type=F32)


def _adaln_kernel(c_ref, w_ref, b_ref, o_ref):
    c = c_ref[...]
    s = c * jax.nn.sigmoid(c)
    w = w_ref[0]
    s_hi = s.astype(BF16)
    s_lo = (s - s_hi.astype(F32)).astype(BF16)
    w_hi = w.astype(BF16)
    w_lo = (w - w_hi.astype(F32)).astype(BF16)
    o_ref[0] = _dot(s_hi, w_hi) + _dot(s_lo, w_hi) + _dot(s_hi, w_lo) + b_ref[0]


def _adaln(c_all, mod_w, mod_b):
    depth, d, n6 = mod_w.shape
    rows = c_all.shape[0]
    tn = n6 // 4
    return pl.pallas_call(
        _adaln_kernel,
        grid=(depth, n6 // tn),
        in_specs=[
            pl.BlockSpec((rows, d), lambda i, j: (0, 0)),
            pl.BlockSpec((1, d, tn), lambda i, j: (i, 0, j)),
            pl.BlockSpec((1, 1, tn), lambda i, j: (i, 0, j)),
        ],
        out_specs=pl.BlockSpec((1, rows, tn), lambda i, j: (i, 0, j)),
        out_shape=jax.ShapeDtypeStruct((depth, rows, n6), F32),
        compiler_params=_params("arbitrary", "arbitrary"),
        name="adaln_mod",
    )(c_all, mod_w, mod_b.reshape(depth, 1, n6))


def _modulated_norm(x, gain, shift, scale):
    return (_rms(x) * gain * (1.0 + scale) + shift).astype(BF16)


def _ret_in_kernel(x_ref, mod_ref, g_ref, w_ref, cos_ref, sin_ref,
                   q_ref, kt_ref, v_ref, gate_ref, h_scr, *, d):
    m = mod_ref[0]
    hw = V7X_MXU_DIM
    half = V7X_LANES
    c = RET_CHUNK
    ts = h_scr.shape[1]
    n_sub = h_scr.shape[0]
    v_off = 2 * d
    g_off = 4 * d

    def normalise(sub):
        rs = slice(sub * ts, (sub + 1) * ts)
        h_scr[sub] = _modulated_norm(x_ref[0, rs, :], g_ref[...], m[0:1], m[1:2])

    def project(sub):
        rs = slice(sub * ts, (sub + 1) * ts)
        cos = cos_ref[rs, :]
        sin = sin_ref[rs, :]
        for hh in range(2 * RET_HEADS):
            acc = _dot(h_scr[sub], w_ref[:, hh * hw:(hh + 1) * hw])
            if hh < RET_HEADS:
                acc = acc * (hw ** -0.5)
            a = acc[:, :half]
            b = acc[:, half:]
            ra = a * cos - b * sin
            rb = a * sin + b * cos
            if hh < RET_HEADS:
                q_ref[0, hh, rs, :half] = ra.astype(BF16)
                q_ref[0, hh, rs, half:] = rb.astype(BF16)
            else:
                for ci in range(ts // c):
                    kc = sub * (ts // c) + ci
                    kt_ref[0, hh - RET_HEADS, kc, :half, :] = ra[ci * c:(ci + 1) * c].T.astype(BF16)
                    kt_ref[0, hh - RET_HEADS, kc, half:, :] = rb[ci * c:(ci + 1) * c].T.astype(BF16)
        per_head = v_ref.shape[-1] // hw
        for cc in range(2 * d // hw):
            cs = slice((cc % per_head) * hw, (cc % per_head + 1) * hw)
            v = _dot(h_scr[sub], w_ref[:, v_off + cc * hw:v_off + (cc + 1) * hw])
            v_ref[0, cc // per_head, rs, cs] = v.astype(BF16)
        for cc in range(2 * d // hw):
            cs = slice((cc % per_head) * hw, (cc % per_head + 1) * hw)
            g = _dot(h_scr[sub], w_ref[:, g_off + cc * hw:g_off + (cc + 1) * hw])
            gate_ref[0, cc // per_head, rs, cs] = (g * jax.nn.sigmoid(g)).astype(BF16)

    for sub in range(n_sub):
        normalise(sub)
    for sub in range(n_sub):
        project(sub)


def _ret_in(x, mod, gain, w, cos, sin):
    b, n, d = x.shape
    tm = RET_TOK_TILE
    c = RET_CHUNK
    dk = d // RET_HEADS
    dv = 2 * d // RET_HEADS
    return pl.pallas_call(
        functools.partial(_ret_in_kernel, d=d),
        grid=(b, n // tm),
        in_specs=[
            pl.BlockSpec((1, tm, d), lambda bi, i: (bi, i, 0)),
            pl.BlockSpec((1, N_MOD, d), lambda bi, i: (bi, 0, 0)),
            _resident(gain.shape),
            _resident(w.shape),
            pl.BlockSpec((tm, V7X_LANES), lambda bi, i: (i, 0)),
            pl.BlockSpec((tm, V7X_LANES), lambda bi, i: (i, 0)),
        ],
        out_specs=[
            pl.BlockSpec((1, RET_HEADS, tm, dk), lambda bi, i: (bi, 0, i, 0)),
            pl.BlockSpec((1, RET_HEADS, tm // c, dk, c), lambda bi, i: (bi, 0, i, 0, 0)),
            pl.BlockSpec((1, RET_HEADS, tm, dv), lambda bi, i: (bi, 0, i, 0)),
            pl.BlockSpec((1, RET_HEADS, tm, dv), lambda bi, i: (bi, 0, i, 0)),
        ],
        out_shape=[
            jax.ShapeDtypeStruct((b, RET_HEADS, n, dk), BF16),
            jax.ShapeDtypeStruct((b, RET_HEADS, n // c, dk, c), BF16),
            jax.ShapeDtypeStruct((b, RET_HEADS, n, dv), BF16),
            jax.ShapeDtypeStruct((b, RET_HEADS, n, dv), BF16),
        ],
        scratch_shapes=[pltpu.VMEM((tm // SUB_TILE, SUB_TILE, d), BF16)],
        compiler_params=_params("arbitrary", "arbitrary"),
        name="ret_in_proj",
    )(x, mod, gain, w, cos, sin)


def _retention_kernel(q_ref, kt_ref, v_ref, g_ref, dec_ref, o_ref,
                      sf_all, sb_all, sf_scr, sb_scr, qdf_scr, qdb_scr, kdf_scr, kdb_scr, dm_scr, *, nc):
    c = RET_CHUNK
    dk = q_ref.shape[-1]
    dv = v_ref.shape[-1]
    dec = dec_ref[0]
    neg = -dec
    log_g = -(jnp.maximum(neg, 0.0) + jnp.log1p(jnp.exp(-jnp.abs(neg))))
    lf = log_g[0:1, :]
    lb = log_g[1:2, :]
    qpos = lax.broadcasted_iota(jnp.int32, (c, dk), 0).astype(F32)
    qdf_scr[...] = jnp.exp(lf[:, :dk] * (qpos + 1.0))
    qdb_scr[...] = jnp.exp(lb[:, :dk] * (c - qpos))
    kpos = lax.broadcasted_iota(jnp.int32, (dk, c), 1).astype(F32)
    kdf_scr[...] = jnp.exp(lf[:, :c] * (c - 1.0 - kpos))
    kdb_scr[...] = jnp.exp(lb[:, :c] * kpos)
    diff = (lax.broadcasted_iota(jnp.int32, (c, c), 0)
            - lax.broadcasted_iota(jnp.int32, (c, c), 1)).astype(F32)
    dm_scr[...] = jnp.where(diff >= 0.0,
                            jnp.exp(lf[:, :c] * jnp.maximum(diff, 0.0)),
                            jnp.exp(lb[:, :c] * jnp.maximum(-diff, 0.0)))
    cdf = jnp.exp(lf[:, :dv] * c)
    cdb = jnp.exp(lb[:, :dv] * c)

    def rows(i):
        return pl.ds(pl.multiple_of(i * c, c), c)

    sf_scr[...] = jnp.zeros_like(sf_scr)
    sb_scr[...] = jnp.zeros_like(sb_scr)
    sf_all[0] = jnp.zeros((dk, dv), BF16)
    sb_all[nc - 1] = jnp.zeros((dk, dv), BF16)

    def scan_body(t, carry):
        kf = (kt_ref[0, 0, t].astype(F32) * kdf_scr[...]).astype(BF16)
        sf = sf_scr[...] * cdf + _dot(kf, v_ref[0, 0, rows(t), :])
        sf_scr[...] = sf
        sf_all[t + 1] = sf.astype(BF16)
        j = nc - 1 - t
        kb = (kt_ref[0, 0, j].astype(F32) * kdb_scr[...]).astype(BF16)
        sb = sb_scr[...] * cdb + _dot(kb, v_ref[0, 0, rows(j), :])
        sb_scr[...] = sb
        sb_all[j - 1] = sb.astype(BF16)
        return carry

    lax.fori_loop(0, nc - 1, scan_body, 0, unroll=SCAN_UNROLL)

    def out_body(i, carry):
        r = rows(i)
        q = q_ref[0, 0, r, :]
        v = v_ref[0, 0, r, :]
        p = (_dot(q, kt_ref[0, 0, i]) * dm_scr[...]).astype(BF16)
        q32 = q.astype(F32)
        qf = (q32 * qdf_scr[...]).astype(BF16)
        qb = (q32 * qdb_scr[...]).astype(BF16)
        y = _dot(p, v) + _dot(qf, sf_all[i]) + _dot(qb, sb_all[i])
        mu = jnp.mean(y, axis=-1, keepdims=True)
        yc = y - mu
        var = jnp.mean(yc * yc, axis=-1, keepdims=True)
        yn = yc * lax.rsqrt(var + EPS)
        o_ref[0, 0, r, :] = (yn * g_ref[0, 0, r, :].astype(F32)).astype(BF16)
        return carry

    lax.fori_loop(0, nc, out_body, 0, unroll=OUT_UNROLL)


def _retention(q, kt, v, gate, dec):
    b, _, n, dk = q.shape
    dv = v.shape[-1]
    c = RET_CHUNK
    nc = n // c
    return pl.pallas_call(
        functools.partial(_retention_kernel, nc=nc),
        grid=(b, RET_HEADS),
        in_specs=[
            pl.BlockSpec((1, 1, n, dk), lambda bi, h: (bi, h, 0, 0)),
            pl.BlockSpec((1, 1, nc, dk, c), lambda bi, h: (bi, h, 0, 0, 0)),
            pl.BlockSpec((1, 1, n, dv), lambda bi, h: (bi, h, 0, 0)),
            pl.BlockSpec((1, 1, n, dv), lambda bi, h: (bi, h, 0, 0)),
            pl.BlockSpec((1, 8, dv), lambda bi, h: (h, 0, 0)),
        ],
        out_specs=pl.BlockSpec((1, 1, n, dv), lambda bi, h: (bi, h, 0, 0)),
        out_shape=jax.ShapeDtypeStruct((b, RET_HEADS, n, dv), BF16),
        scratch_shapes=[
            pltpu.VMEM((nc, dk, dv), BF16),
            pltpu.VMEM((nc, dk, dv), BF16),
            pltpu.VMEM((dk, dv), F32),
            pltpu.VMEM((dk, dv), F32),
            pltpu.VMEM((c, dk), F32),
            pltpu.VMEM((c, dk), F32),
            pltpu.VMEM((dk, c), F32),
            pltpu.VMEM((dk, c), F32),
            pltpu.VMEM((c, c), F32),
        ],
        compiler_params=_params("arbitrary", "arbitrary"),
        name="retention_core",
    )(q, kt, v, gate, dec)


def _out_mlp_kernel(x_ref, y_ref, mod_ref, wo_ref, g2_ref, w1_ref, w2_ref, fg_ref, o_ref,
                    h_scr, *, final):
    m = mod_ref[0]
    ts = h_scr.shape[1]
    n_sub = h_scr.shape[0]
    d_ff = w1_ref.shape[1]

    def mix(sub):
        rs = slice(sub * ts, (sub + 1) * ts)
        dy = y_ref.shape[-1]
        mixed = _dot(y_ref[0, 0, rs, :], wo_ref[0:dy, :])
        for hy in range(1, y_ref.shape[1]):
            mixed += _dot(y_ref[0, hy, rs, :], wo_ref[hy * dy:(hy + 1) * dy, :])
        x1 = x_ref[0, rs, :] + m[2:3] * mixed
        o_ref[0, rs, :] = x1
        h_scr[sub] = _modulated_norm(x1, g2_ref[...], m[3:4], m[4:5])

    def mlp(sub):
        rs = slice(sub * ts, (sub + 1) * ts)
        for cc in range(d_ff // FF_CHUNK):
            cs = slice(cc * FF_CHUNK, (cc + 1) * FF_CHUNK)
            u = jnp.maximum(_dot(h_scr[sub], w1_ref[:, cs]), 0.0)
            o_ref[0, rs, :] += m[5:6] * _dot((u * u).astype(BF16), w2_ref[cs, :])
        if final:
            o_ref[0, rs, :] = _rms(o_ref[0, rs, :]) * fg_ref[...]

    for sub in range(n_sub):
        mix(sub)
    for sub in range(n_sub):
        mlp(sub)


def _out_mlp(x, y, mod, w_out, gain2, w1, w2, final_g, final):
    b, n, d = x.shape
    tm = MLP_TOK_TILE
    _, hy, _, dy = y.shape
    return pl.pallas_call(
        functools.partial(_out_mlp_kernel, final=final),
        grid=(b, n // tm),
        in_specs=[
            pl.BlockSpec((1, tm, d), lambda bi, i: (bi, i, 0)),
            pl.BlockSpec((1, hy, tm, dy), lambda bi, i: (bi, 0, i, 0)),
            pl.BlockSpec((1, N_MOD, d), lambda bi, i: (bi, 0, 0)),
            _resident(w_out.shape),
            _resident(gain2.shape),
            _resident(w1.shape),
            _resident(w2.shape),
            _resident(final_g.shape),
        ],
        out_specs=pl.BlockSpec((1, tm, d), lambda bi, i: (bi, i, 0)),
        out_shape=jax.ShapeDtypeStruct((b, n, d), F32),
        scratch_shapes=[pltpu.VMEM((tm // SUB_TILE, SUB_TILE, d), BF16)],
        compiler_params=_params("arbitrary", "arbitrary"),
        name="out_proj_mlp_final" if final else "out_proj_mlp",
    )(x, y, mod, w_out, gain2, w1, w2, final_g)


def _att_in_kernel(x_ref, mod_ref, g_ref, w_ref, qga_ref, qgb_ref, kga_ref, kgb_ref, cos_ref, sin_ref,
                   q_ref, kt_ref, vx_ref, h_scr, *, d):
    m = mod_ref[0]
    h_scr[...] = _modulated_norm(x_ref[0], g_ref[...], m[0:1], m[1:2])
    hd = ATT_HD
    hw = V7X_MXU_DIM
    half = hw // 2
    part = hd // 2
    n_q_slabs = d // hw
    cos = cos_ref[...]
    sin = sin_ref[...]

    def tables(ga, gb):
        return cos * ga, sin * gb, sin * ga, cos * gb

    q_scale = hd ** -0.5 * LOG2_E
    q_tab = tables(qga_ref[...] * q_scale, qgb_ref[...] * q_scale)
    k_tab = tables(kga_ref[...], kgb_ref[...])
    first = lax.broadcasted_iota(jnp.int32, (1, half), 1) < part
    for sl in range(n_q_slabs + 1):
        acc = _dot(h_scr[...], w_ref[:, sl * hw:(sl + 1) * hw])
        x = acc[:, :half]
        y = acc[:, half:]
        u = x * x + y * y
        ss0 = jnp.sum(jnp.where(first, u, 0.0), axis=-1, keepdims=True)
        ss1 = jnp.sum(jnp.where(first, 0.0, u), axis=-1, keepdims=True)
        r = lax.rsqrt(jnp.where(first, ss0, ss1) * (1.0 / hd) + EPS)
        c_a, s_b, s_a, c_b = q_tab if sl < n_q_slabs else k_tab
        ox = (x * c_a - y * s_b) * r
        oy = (x * s_a + y * c_b) * r
        if sl < n_q_slabs:
            q_ref[0, :, sl * hw:sl * hw + half] = ox.astype(BF16)
            q_ref[0, :, sl * hw + half:(sl + 1) * hw] = oy.astype(BF16)
        else:
            xt = ox.T.astype(BF16)
            yt = oy.T.astype(BF16)
            for g in range(ATT_KV_HEADS):
                a = xt[g * part:(g + 1) * part]
                b = yt[g * part:(g + 1) * part]
                kt_ref[0, g, 0 * part:1 * part] = a
                kt_ref[0, g, 1 * part:2 * part] = a
                kt_ref[0, g, 2 * part:3 * part] = b
                kt_ref[0, g, 3 * part:4 * part] = b
    v_off = (n_q_slabs + 1) * hw
    v = _dot(h_scr[...], w_ref[:, v_off:v_off + ATT_KV_HEADS * hd])
    for g in range(ATT_KV_HEADS):
        vx_ref[0, :, 2 * g * hd:(2 * g + 1) * hd] = v[:, g * hd:(g + 1) * hd].astype(BF16)
        vx_ref[0, :, (2 * g + 1) * hd:(2 * g + 2) * hd] = jnp.ones((x_ref.shape[1], hd), BF16)


def _att_in(x, mod, gain, w, q_gain_a, q_gain_b, k_gain_a, k_gain_b, cos, sin):
    b, n, d = x.shape
    tm = ATT_TOK_TILE
    hd = ATT_HD
    assert ATT_KV_HEADS * hd == V7X_MXU_DIM
    return pl.pallas_call(
        functools.partial(_att_in_kernel, d=d),
        grid=(b, n // tm),
        in_specs=[
            pl.BlockSpec((1, tm, d), lambda bi, i: (bi, i, 0)),
            pl.BlockSpec((1, N_MOD, d), lambda bi, i: (bi, 0, 0)),
            _resident(gain.shape),
            _resident(w.shape),
            _resident(q_gain_a.shape),
            _resident(q_gain_b.shape),
            _resident(k_gain_a.shape),
            _resident(k_gain_b.shape),
            pl.BlockSpec((tm, hd), lambda bi, i: (i, 0)),
            pl.BlockSpec((tm, hd), lambda bi, i: (i, 0)),
        ],
        out_specs=[
            pl.BlockSpec((1, tm, d), lambda bi, i: (bi, i, 0)),
            pl.BlockSpec((1, ATT_KV_HEADS, 2 * hd, tm), lambda bi, i: (bi, 0, 0, i)),
            pl.BlockSpec((1, tm, 2 * ATT_KV_HEADS * hd), lambda bi, i: (bi, i, 0)),
        ],
        out_shape=[
            jax.ShapeDtypeStruct((b, n, d), BF16),
            jax.ShapeDtypeStruct((b, ATT_KV_HEADS, 2 * hd, n), BF16),
            jax.ShapeDtypeStruct((b, n, 2 * ATT_KV_HEADS * hd), BF16),
        ],
        scratch_shapes=[pltpu.VMEM((tm, d), BF16)],
        compiler_params=_params("arbitrary", "arbitrary"),
        name="att_in_proj",
    )(x, mod, gain, w, q_gain_a, q_gain_b, k_gain_a, k_gain_b, cos, sin)


def _attention_kernel(q_ref, kt_ref, vx_ref, o_ref, s_scr, p_scr, *, group, tq):
    hd = ATT_HD
    nt = q_ref.shape[1] // tq

    def rows(i):
        return pl.ds(pl.multiple_of(i * tq, tq), tq)

    hw = V7X_MXU_DIM
    heads_per_slab = hw // hd
    lane_head = (lax.broadcasted_iota(jnp.int32, (1, hw), 1) // (hd // 2)) % heads_per_slab

    def scores(i, g):
        slab = q_ref[0, rows(i), (g // heads_per_slab) * hw:(g // heads_per_slab + 1) * hw]
        mine = lane_head == g % heads_per_slab
        s_scr[g % 2] = _dot(jnp.where(mine, slab, jnp.zeros_like(slab)), kt_ref[0, 0])

    def probs(g):
        m = jnp.max(s_scr[g % 2], axis=-1, keepdims=True)
        p_scr[g % 2] = jnp.exp2(s_scr[g % 2] - m).astype(BF16)

    def weighted(i, g):
        acc = _dot(p_scr[g % 2], vx_ref[0])
        o_ref[0, 0, rows(i), g * hd:(g + 1) * hd] = (acc[:, :hd] / acc[:, hd:]).astype(BF16)

    scores(0, 0)

    def body(i, carry):
        for g in range(group):
            probs(g)
            if g + 1 < group:
                scores(i, g + 1)
            else:
                scores(jnp.minimum(i + 1, nt - 1), 0)
            weighted(i, g)
        return carry

    lax.fori_loop(0, nt, body, 0, unroll=ATT_TILE_UNROLL)


def _attention(q, kt, vx):
    b, n, d = q.shape
    hd = ATT_HD
    group = d // hd // ATT_KV_HEADS
    tq = ATT_Q_TILE
    assert group % 2 == 0
    return pl.pallas_call(
        functools.partial(_attention_kernel, group=group, tq=tq),
        grid=(b, ATT_KV_HEADS),
        in_specs=[
            pl.BlockSpec((1, n, group * hd), lambda bi, g: (bi, 0, g)),
            pl.BlockSpec((1, 1, 2 * hd, n), lambda bi, g: (bi, g, 0, 0)),
            pl.BlockSpec((1, n, 2 * hd), lambda bi, g: (bi, 0, g)),
        ],
        out_specs=pl.BlockSpec((1, 1, n, group * hd), lambda bi, g: (bi, g, 0, 0)),
        out_shape=jax.ShapeDtypeStruct((b, ATT_KV_HEADS, n, group * hd), BF16),
        scratch_shapes=[pltpu.VMEM((2, tq, n), F32), pltpu.VMEM((2, tq, n), BF16)],
        compiler_params=_params("arbitrary", "arbitrary"),
        name="gqa_attention",
    )(q, kt, vx)


def _rope_angles(n, head_dim):
    rows = n // GRID_W
    row = jnp.repeat(jnp.arange(rows, dtype=F32), GRID_W)
    col = jnp.tile(jnp.arange(GRID_W, dtype=F32), rows)
    nf = head_dim // 4
    inv = ROPE_THETA ** (-jnp.arange(nf, dtype=F32) / nf)
    return row[:, None] * inv[None, :], col[:, None] * inv[None, :]


def _halves_first(head_dim):
    q = head_dim // 4
    return jnp.concatenate([jnp.arange(0, q), jnp.arange(2 * q, 3 * q),
                            jnp.arange(q, 2 * q), jnp.arange(3 * q, 4 * q)])


def _trunk(x, mod, p):
    q, kt, v, gate = _ret_in(x, mod[0], p["norm1_g"][0], p["ret_w_in"], p["ret_cos"], p["ret_sin"])
    y = _retention(q, kt, v, gate, p["ret_dec"])
    x = _out_mlp(x, y, mod[0], p["ret_w_out"], p["norm2_g"][0], p["mlp_w1"][0], p["mlp_w2"][0],
                 p["final_g"], final=False)
    q, kt, vx = _att_in(x, mod[1], p["norm1_g"][1], p["att_w_in"], p["att_q_gain_a"], p["att_q_gain_b"],
                        p["att_k_gain_a"], p["att_k_gain_b"], p["att_cos"], p["att_sin"])
    y = _attention(q, kt, vx)
    return _out_mlp(x, y, mod[1], p["att_w_out"], p["norm2_g"][1], p["mlp_w1"][1], p["mlp_w2"][1],
                    p["final_g"], final=True)


def kernel(x_prompt, x_sample, c_prompt, c_sample, mod_w, mod_b, norm1_g, norm2_g, ret_w_in, ret_decay,
           ret_w_out, att_w_in, att_q_gain, att_k_gain, att_w_out, mlp_w1, mlp_w2, final_g):
    depth, d, _ = mod_w.shape
    assert depth == 2 and ret_w_in.shape[0] == 1 and att_w_in.shape[0] == 1
    bp, n, _ = x_prompt.shape
    bs = x_sample.shape[0]
    assert x_sample.shape[1] == n

    c_all = jnp.concatenate([c_prompt, c_sample], axis=0)
    pad = (-c_all.shape[0]) % 16
    c_all = jnp.pad(c_all, ((0, pad), (0, 0)))
    mod = _adaln(c_all, mod_w, mod_b).reshape(depth, -1, N_MOD, d)

    dk = d // RET_HEADS
    perm_r = _halves_first(dk)
    qk_cols = (jnp.arange(2 * RET_HEADS)[:, None] * dk + perm_r[None, :]).reshape(-1)
    ret_cols = jnp.concatenate([qk_cols, jnp.arange(2 * d, ret_w_in.shape[2])])
    part_idx = _halves_first(ATT_HD).reshape(2, ATT_HD // 2)
    n_qk = d // ATT_HD + ATT_KV_HEADS
    slab_heads = jnp.arange(n_qk).reshape(-1, 2)
    att_qk_cols = (slab_heads[:, None, :, None] * ATT_HD + part_idx[None, :, None, :]).reshape(-1)
    att_cols = jnp.concatenate([att_qk_cols, jnp.arange(n_qk * ATT_HD, att_w_in.shape[2])])

    ar, ac = _rope_angles(n, dk)
    br, bc = _rope_angles(n, ATT_HD)
    dv = 2 * d // RET_HEADS

    def slab_gain(gain, part):
        return jnp.tile(gain[part_idx[part]], 2).reshape(1, ATT_HD)
    p = {
        "norm1_g": norm1_g.reshape(depth, 1, d),
        "norm2_g": norm2_g.reshape(depth, 1, d),
        "final_g": final_g.reshape(1, d),
        "ret_w_in": ret_w_in[0][:, ret_cols].astype(BF16),
        "ret_w_out": ret_w_out[0].astype(BF16),
        "att_w_in": att_w_in[0][:, att_cols].astype(BF16),
        "att_w_out": att_w_out[0].astype(BF16),
        "att_q_gain_a": slab_gain(att_q_gain[0], 0),
        "att_q_gain_b": slab_gain(att_q_gain[0], 1),
        "att_k_gain_a": slab_gain(att_k_gain[0], 0),
        "att_k_gain_b": slab_gain(att_k_gain[0], 1),
        "mlp_w1": mlp_w1.astype(BF16),
        "mlp_w2": mlp_w2.astype(BF16),
        "ret_cos": jnp.concatenate([jnp.cos(ar), jnp.cos(ac)], axis=-1),
        "ret_sin": jnp.concatenate([jnp.sin(ar), jnp.sin(ac)], axis=-1),
        "att_cos": jnp.concatenate([jnp.cos(br), jnp.cos(bc), jnp.cos(br), jnp.cos(bc)], axis=-1),
        "att_sin": jnp.concatenate([jnp.sin(br), jnp.sin(bc), jnp.sin(br), jnp.sin(bc)], axis=-1),
        "ret_dec": jnp.pad(jnp.broadcast_to(ret_decay[0].T[:, :, None], (RET_HEADS, 2, dv)),
                           ((0, 0), (0, 6), (0, 0))),
    }
    y_prompt = _trunk(x_prompt, mod[:, :bp], p)
    y_sample = _trunk(x_sample, mod[:, bp:bp + bs], p)
    return (y_prompt, y_sample)
```

```python
import functools
from typing import NamedTuple

import jax
import jax.numpy as jnp
from jax import lax
from jax.experimental import pallas as pl
from jax.experimental.pallas import tpu as pltpu

F32 = jnp.float32
BF16 = jnp.bfloat16

GRID_W = 64
ROPE_THETA = 10000.0
EPS = 1e-6
RET_HEADS = 4
ATT_HD = 128
ATT_KV_HEADS = 2
N_MOD = 6

V7X_VMEM_BYTES = 64 * 1024 * 1024
V7X_LANES = 128
V7X_MXU_DIM = 256

VMEM_LIMIT = V7X_VMEM_BYTES - 8 * 1024 * 1024
RET_TOK_TILE = 1024
SUB_TILE = 512
ATT_TOK_TILE = 1024
MLP_TOK_TILE = 1024
FF_CHUNK = 1024
RET_CHUNK = V7X_MXU_DIM
SCAN_UNROLL = 5
OUT_UNROLL = 8
ATT_Q_TILE = 512
ATT_TILE_UNROLL = 2
LOG2_E = 1.4426950408889634


def _params(*sem):
    return pltpu.CompilerParams(dimension_semantics=sem, vmem_limit_bytes=VMEM_LIMIT)


def _resident(shape):
    nd = len(shape)
    return pl.BlockSpec(shape, lambda *_: (0,) * nd, pipeline_mode=pl.Buffered(1))


def _rms(x):
    return x * lax.rsqrt(jnp.mean(x * x, axis=-1, keepdims=True) + EPS)


def _dot(a, b):
    return jnp.dot(a, b, preferred_element_type=F32)


def _adaln_kernel(c_ref, w_ref, b_ref, o_ref):
    c = c_ref[...]
    s = c * jax.nn.sigmoid(c)
    w = w_ref[0]
    s_hi = s.astype(BF16)
    s_lo = (s - s_hi.astype(F32)).astype(BF16)
    w_hi = w.astype(BF16)
    w_lo = (w - w_hi.astype(F32)).astype(BF16)
    o_ref[0] = _dot(s_hi, w_hi) + _dot(s_lo, w_hi) + _dot(s_hi, w_lo) + b_ref[0]


def _adaln(c_all, mod_w, mod_b):
    depth, d, n6 = mod_w.shape
    rows = c_all.shape[0]
    tn = n6 // 4
    return pl.pallas_call(
        _adaln_kernel,
        grid=(depth, n6 // tn),
        in_specs=[
            pl.BlockSpec((rows, d), lambda i, j: (0, 0)),
            pl.BlockSpec((1, d, tn), lambda i, j: (i, 0, j)),
            pl.BlockSpec((1, 1, tn), lambda i, j: (i, 0, j)),
        ],
        out_specs=pl.BlockSpec((1, rows, tn), lambda i, j: (i, 0, j)),
        out_shape=jax.ShapeDtypeStruct((depth, rows, n6), F32),
        compiler_params=_params("arbitrary", "arbitrary"),
        name="adaln_mod",
    )(c_all, mod_w, mod_b.reshape(depth, 1, n6))


class ModRows(NamedTuple):
    table: jax.Array
    layer: int
    first_row: int


def _mod_spec(mod):
    _, _, n_mod, d = mod.table.shape
    return pl.BlockSpec((1, 1, n_mod, d), lambda bi, i: (mod.layer, mod.first_row + bi, 0, 0))


def _modulated_norm(x, gain, shift, scale):
    return (_rms(x) * gain * (1.0 + scale) + shift).astype(BF16)


def _ret_in_kernel(x_ref, mod_ref, g_ref, w_ref, cos_ref, sin_ref,
                   q_ref, kt_ref, v_ref, gate_ref, h_scr, *, d):
    m = mod_ref[0, 0]
    hw = V7X_MXU_DIM
    half = V7X_LANES
    c = RET_CHUNK
    ts = h_scr.shape[1]
    n_sub = h_scr.shape[0]
    v_off = 2 * d
    g_off = 4 * d

    def normalise(sub):
        rs = slice(sub * ts, (sub + 1) * ts)
        h_scr[sub] = _modulated_norm(x_ref[0, rs, :], g_ref[...], m[0:1], m[1:2])

    def project(sub):
        rs = slice(sub * ts, (sub + 1) * ts)
        cos = cos_ref[rs, :]
        sin = sin_ref[rs, :]
        for hh in list(range(RET_HEADS, 2 * RET_HEADS)) + list(range(RET_HEADS)):
            acc = _dot(h_scr[sub], w_ref[:, hh * hw:(hh + 1) * hw])
            if hh < RET_HEADS:
                acc = acc * (hw ** -0.5)
            a = acc[:, :half]
            b = acc[:, half:]
            ra = a * cos - b * sin
            rb = a * sin + b * cos
            if hh < RET_HEADS:
                q_ref[0, hh, rs, :half] = ra.astype(BF16)
                q_ref[0, hh, rs, half:] = rb.astype(BF16)
            else:
                for ci in range(ts // c):
                    kc = sub * (ts // c) + ci
                    kt_ref[0, hh - RET_HEADS, kc, :half, :] = ra[ci * c:(ci + 1) * c].T.astype(BF16)
                    kt_ref[0, hh - RET_HEADS, kc, half:, :] = rb[ci * c:(ci + 1) * c].T.astype(BF16)
        dv = v_ref.shape[-1]
        for hh in range(RET_HEADS):
            g = _dot(h_scr[sub], w_ref[:, g_off + hh * dv:g_off + (hh + 1) * dv])
            gate_ref[0, hh, rs, :] = (g * jax.nn.sigmoid(g)).astype(BF16)
        for hh in range(RET_HEADS):
            v = _dot(h_scr[sub], w_ref[:, v_off + hh * dv:v_off + (hh + 1) * dv])
            v_ref[0, hh, rs, :] = v.astype(BF16)

    for sub in range(n_sub):
        normalise(sub)
    for sub in range(n_sub):
        project(sub)


def _ret_in(x, mod, gain, w, cos, sin):
    b, n, d = x.shape
    tm = RET_TOK_TILE
    c = RET_CHUNK
    dk = d // RET_HEADS
    dv = 2 * d // RET_HEADS
    return pl.pallas_call(
        functools.partial(_ret_in_kernel, d=d),
        grid=(b, n // tm),
        in_specs=[
            pl.BlockSpec((1, tm, d), lambda bi, i: (bi, i, 0)),
            _mod_spec(mod),
            _resident(gain.shape),
            _resident(w.shape),
            pl.BlockSpec((tm, V7X_LANES), lambda bi, i: (i, 0)),
            pl.BlockSpec((tm, V7X_LANES), lambda bi, i: (i, 0)),
        ],
        out_specs=[
            pl.BlockSpec((1, RET_HEADS, tm, dk), lambda bi, i: (bi, 0, i, 0)),
            pl.BlockSpec((1, RET_HEADS, tm // c, dk, c), lambda bi, i: (bi, 0, i, 0, 0)),
            pl.BlockSpec((1, RET_HEADS, tm, dv), lambda bi, i: (bi, 0, i, 0)),
            pl.BlockSpec((1, RET_HEADS, tm, dv), lambda bi, i: (bi, 0, i, 0)),
        ],
        out_shape=[
            jax.ShapeDtypeStruct((b, RET_HEADS, n, dk), BF16),
            jax.ShapeDtypeStruct((b, RET_HEADS, n // c, dk, c), BF16),
            jax.ShapeDtypeStruct((b, RET_HEADS, n, dv), BF16),
            jax.ShapeDtypeStruct((b, RET_HEADS, n, dv), BF16),
        ],
        scratch_shapes=[pltpu.VMEM((tm // SUB_TILE, SUB_TILE, d), BF16)],
        compiler_params=_params("arbitrary", "arbitrary"),
        name="ret_in_proj",
    )(x, mod.table, gain, w, cos, sin)


def _retention_kernel(q_ref, kt_ref, v_ref, dec_ref, o_ref,
                      sf_all, sb_all, sf_scr, sb_scr, qdf_scr, qdb_scr, kdf_scr, kdb_scr, dm_scr, *, nc):
    c = RET_CHUNK
    dk = q_ref.shape[-1]
    dv = v_ref.shape[-1]
    dec = dec_ref[0]
    neg = -dec
    log_g = -(jnp.maximum(neg, 0.0) + jnp.log1p(jnp.exp(-jnp.abs(neg))))
    lf = log_g[0:1, :]
    lb = log_g[1:2, :]
    qpos = lax.broadcasted_iota(jnp.int32, (c, dk), 0).astype(F32)
    qdf_scr[...] = jnp.exp(lf[:, :dk] * (qpos + 1.0))
    qdb_scr[...] = jnp.exp(lb[:, :dk] * (c - qpos))
    kpos = lax.broadcasted_iota(jnp.int32, (dk, c), 1).astype(F32)
    kdf_scr[...] = jnp.exp(lf[:, :c] * (c - 1.0 - kpos))
    kdb_scr[...] = jnp.exp(lb[:, :c] * kpos)
    diff = (lax.broadcasted_iota(jnp.int32, (c, c), 0)
            - lax.broadcasted_iota(jnp.int32, (c, c), 1)).astype(F32)
    dm_scr[...] = jnp.where(diff >= 0.0,
                            jnp.exp(lf[:, :c] * jnp.maximum(diff, 0.0)),
                            jnp.exp(lb[:, :c] * jnp.maximum(-diff, 0.0)))
    cdf = jnp.exp(lf[:, :dv] * c)
    cdb = jnp.exp(lb[:, :dv] * c)

    def rows(i):
        return pl.ds(pl.multiple_of(i * c, c), c)

    sf_scr[...] = jnp.zeros_like(sf_scr)
    sb_scr[...] = jnp.zeros_like(sb_scr)
    sf_all[0] = jnp.zeros((dk, dv), BF16)
    sb_all[nc - 1] = jnp.zeros((dk, dv), BF16)

    def scan_body(t, carry):
        kf = (kt_ref[0, 0, t].astype(F32) * kdf_scr[...]).astype(BF16)
        sf = sf_scr[...] * cdf + _dot(kf, v_ref[0, 0, rows(t), :])
        sf_scr[...] = sf
        sf_all[t + 1] = sf.astype(BF16)
        j = nc - 1 - t
        kb = (kt_ref[0, 0, j].astype(F32) * kdb_scr[...]).astype(BF16)
        sb = sb_scr[...] * cdb + _dot(kb, v_ref[0, 0, rows(j), :])
        sb_scr[...] = sb
        sb_all[j - 1] = sb.astype(BF16)
        return carry

    lax.fori_loop(0, nc - 1, scan_body, 0, unroll=SCAN_UNROLL)

    def out_body(i, carry):
        r = rows(i)
        q = q_ref[0, 0, r, :]
        v = v_ref[0, 0, r, :]
        p = (_dot(q, kt_ref[0, 0, i]) * dm_scr[...]).astype(BF16)
        q32 = q.astype(F32)
        qf = (q32 * qdf_scr[...]).astype(BF16)
        qb = (q32 * qdb_scr[...]).astype(BF16)
        y = _dot(p, v) + _dot(qf, sf_all[i]) + _dot(qb, sb_all[i])
        mu = jnp.mean(y, axis=-1, keepdims=True)
        yc = y - mu
        var = jnp.mean(yc * yc, axis=-1, keepdims=True)
        yn = yc * lax.rsqrt(var + EPS)
        o_ref[0, 0, r, :] = yn.astype(BF16)
        return carry

    lax.fori_loop(0, nc, out_body, 0, unroll=OUT_UNROLL)


def _retention(q, kt, v, dec):
    b, _, n, dk = q.shape
    dv = v.shape[-1]
    c = RET_CHUNK
    nc = n // c
    return pl.pallas_call(
        functools.partial(_retention_kernel, nc=nc),
        grid=(b, RET_HEADS),
        in_specs=[
            pl.BlockSpec((1, 1, n, dk), lambda bi, h: (bi, h, 0, 0)),
            pl.BlockSpec((1, 1, nc, dk, c), lambda bi, h: (bi, h, 0, 0, 0)),
            pl.BlockSpec((1, 1, n, dv), lambda bi, h: (bi, h, 0, 0)),
            pl.BlockSpec((1, 8, dv), lambda bi, h: (h, 0, 0)),
        ],
        out_specs=pl.BlockSpec((1, 1, n, dv), lambda bi, h: (bi, h, 0, 0)),
        out_shape=jax.ShapeDtypeStruct((b, RET_HEADS, n, dv), BF16),
        scratch_shapes=[
            pltpu.VMEM((nc, dk, dv), BF16),
            pltpu.VMEM((nc, dk, dv), BF16),
            pltpu.VMEM((dk, dv), F32),
            pltpu.VMEM((dk, dv), F32),
            pltpu.VMEM((c, dk), F32),
            pltpu.VMEM((c, dk), F32),
            pltpu.VMEM((dk, c), F32),
            pltpu.VMEM((dk, c), F32),
            pltpu.VMEM((c, c), F32),
        ],
        compiler_params=_params("arbitrary", "arbitrary"),
        name="retention_core",
    )(q, kt, v, dec)


def _out_mlp_kernel(x_ref, y_ref, gate_ref, mod_ref, wo_ref, g2_ref, w1_ref, w2_ref, fg_ref, o_ref,
                    h_scr, *, final, gated):
    m = mod_ref[0, 0]
    ts = h_scr.shape[1]
    n_sub = h_scr.shape[0]
    d_ff = w1_ref.shape[1]

    def mix(sub):
        rs = slice(sub * ts, (sub + 1) * ts)
        dy = y_ref.shape[-1]
        mixed = None
        for hy in range(y_ref.shape[1]):
            yh = y_ref[0, hy, rs, :]
            if gated:
                yh = yh * gate_ref[0, hy, rs, :]
            part = _dot(yh, wo_ref[hy * dy:(hy + 1) * dy, :])
            mixed = part if mixed is None else mixed + part
        x1 = x_ref[0, rs, :] + m[2:3] * mixed
        o_ref[0, rs, :] = x1
        h_scr[sub] = _modulated_norm(x1, g2_ref[...], m[3:4], m[4:5])

    def mlp(sub):
        rs = slice(sub * ts, (sub + 1) * ts)
        for cc in range(d_ff // FF_CHUNK):
            cs = slice(cc * FF_CHUNK, (cc + 1) * FF_CHUNK)
            u = jnp.maximum(_dot(h_scr[sub], w1_ref[:, cs]), 0.0)
            o_ref[0, rs, :] += m[5:6] * _dot((u * u).astype(BF16), w2_ref[cs, :])
        if final:
            o_ref[0, rs, :] = _rms(o_ref[0, rs, :]) * fg_ref[...]

    for sub in range(n_sub):
        mix(sub)
    for sub in range(n_sub):
        mlp(sub)


def _out_mlp(x, y, gate, mod, w_out, gain2, w1, w2, final_g, final):
    b, n, d = x.shape
    tm = MLP_TOK_TILE
    _, hy, _, dy = y.shape
    y_spec = pl.BlockSpec((1, hy, tm, dy), lambda bi, i: (bi, 0, i, 0))
    gated = gate is not None
    if not gated:
        gate = jnp.zeros((1, 1, 8, V7X_LANES), BF16)
    gate_spec = y_spec if gated else _resident(gate.shape)
    return pl.pallas_call(
        functools.partial(_out_mlp_kernel, final=final, gated=gated),
        grid=(b, n // tm),
        in_specs=[
            pl.BlockSpec((1, tm, d), lambda bi, i: (bi, i, 0)),
            y_spec,
            gate_spec,
            _mod_spec(mod),
            _resident(w_out.shape),
            _resident(gain2.shape),
            _resident(w1.shape),
            _resident(w2.shape),
            _resident(final_g.shape),
        ],
        out_specs=pl.BlockSpec((1, tm, d), lambda bi, i: (bi, i, 0)),
        out_shape=jax.ShapeDtypeStruct((b, n, d), F32),
        scratch_shapes=[pltpu.VMEM((tm // SUB_TILE, SUB_TILE, d), BF16)],
        compiler_params=_params("arbitrary", "arbitrary"),
        name="out_proj_mlp_final" if final else "out_proj_mlp",
    )(x, y, gate, mod.table, w_out, gain2, w1, w2, final_g)


def _att_in_kernel(x_ref, mod_ref, g_ref, w_ref, qga_ref, qgb_ref, kga_ref, kgb_ref, cos_ref, sin_ref,
                   q_ref, kt_ref, vx_ref, h_scr, *, d):
    m = mod_ref[0, 0]
    h_scr[...] = _modulated_norm(x_ref[0], g_ref[...], m[0:1], m[1:2])
    hd = ATT_HD
    hw = V7X_MXU_DIM
    half = hw // 2
    part = hd // 2
    n_q_slabs = d // hw
    cos = cos_ref[...]
    sin = sin_ref[...]

    def tables(ga, gb):
        return cos * ga, sin * gb, sin * ga, cos * gb

    q_scale = hd ** -0.5 * LOG2_E
    q_tab = tables(qga_ref[...] * q_scale, qgb_ref[...] * q_scale)
    k_tab = tables(kga_ref[...], kgb_ref[...])
    first = lax.broadcasted_iota(jnp.int32, (1, half), 1) < part
    for sl in [n_q_slabs] + list(range(n_q_slabs)):
        acc = _dot(h_scr[...], w_ref[:, sl * hw:(sl + 1) * hw])
        x = acc[:, :half]
        y = acc[:, half:]
        u = x * x + y * y
        ss0 = jnp.sum(jnp.where(first, u, 0.0), axis=-1, keepdims=True)
        ss1 = jnp.sum(jnp.where(first, 0.0, u), axis=-1, keepdims=True)
        r = lax.rsqrt(jnp.where(first, ss0, ss1) * (1.0 / hd) + EPS)
        c_a, s_b, s_a, c_b = q_tab if sl < n_q_slabs else k_tab
        ox = (x * c_a - y * s_b) * r
        oy = (x * s_a + y * c_b) * r
        if sl < n_q_slabs:
            q_ref[0, :, sl * hw:sl * hw + half] = ox.astype(BF16)
            q_ref[0, :, sl * hw + half:(sl + 1) * hw] = oy.astype(BF16)
        else:
            xt = ox.T.astype(BF16)
            yt = oy.T.astype(BF16)
            for g in range(ATT_KV_HEADS):
                a = xt[g * part:(g + 1) * part]
                b = yt[g * part:(g + 1) * part]
                kt_ref[0, g, 0 * part:1 * part] = a
                kt_ref[0, g, 1 * part:2 * part] = a
                kt_ref[0, g, 2 * part:3 * part] = b
                kt_ref[0, g, 3 * part:4 * part] = b
    v_off = (n_q_slabs + 1) * hw
    v = _dot(h_scr[...], w_ref[:, v_off:v_off + ATT_KV_HEADS * hd])
    for g in range(ATT_KV_HEADS):
        vx_ref[0, :, 2 * g * hd:(2 * g + 1) * hd] = v[:, g * hd:(g + 1) * hd].astype(BF16)
        vx_ref[0, :, (2 * g + 1) * hd:(2 * g + 2) * hd] = jnp.ones((x_ref.shape[1], hd), BF16)


def _att_in(x, mod, gain, w, q_gain_a, q_gain_b, k_gain_a, k_gain_b, cos, sin):
    b, n, d = x.shape
    tm = ATT_TOK_TILE
    hd = ATT_HD
    assert ATT_KV_HEADS * hd == V7X_MXU_DIM
    return pl.pallas_call(
        functools.partial(_att_in_kernel, d=d),
        grid=(b, n // tm),
        in_specs=[
            pl.BlockSpec((1, tm, d), lambda bi, i: (bi, i, 0)),
            _mod_spec(mod),
            _resident(gain.shape),
            _resident(w.shape),
            _resident(q_gain_a.shape),
            _resident(q_gain_b.shape),
            _resident(k_gain_a.shape),
            _resident(k_gain_b.shape),
            pl.BlockSpec((tm, hd), lambda bi, i: (i, 0)),
            pl.BlockSpec((tm, hd), lambda bi, i: (i, 0)),
        ],
        out_specs=[
            pl.BlockSpec((1, tm, d), lambda bi, i: (bi, i, 0)),
            pl.BlockSpec((1, ATT_KV_HEADS, 2 * hd, tm), lambda bi, i: (bi, 0, 0, i)),
            pl.BlockSpec((1, tm, 2 * ATT_KV_HEADS * hd), lambda bi, i: (bi, i, 0)),
        ],
        out_shape=[
            jax.ShapeDtypeStruct((b, n, d), BF16),
            jax.ShapeDtypeStruct((b, ATT_KV_HEADS, 2 * hd, n), BF16),
            jax.ShapeDtypeStruct((b, n, 2 * ATT_KV_HEADS * hd), BF16),
        ],
        scratch_shapes=[pltpu.VMEM((tm, d), BF16)],
        compiler_params=_params("arbitrary", "arbitrary"),
        name="att_in_proj",
    )(x, mod.table, gain, w, q_gain_a, q_gain_b, k_gain_a, k_gain_b, cos, sin)


def _attention_kernel(q_ref, kt_ref, vx_ref, o_ref, s_scr, p_scr, *, group, tq):
    hd = ATT_HD
    nt = q_ref.shape[1] // tq

    def rows(i):
        return pl.ds(pl.multiple_of(i * tq, tq), tq)

    hw = V7X_MXU_DIM
    heads_per_slab = hw // hd
    lane_head = (lax.broadcasted_iota(jnp.int32, (1, hw), 1) // (hd // 2)) % heads_per_slab

    def scores(i, g):
        slab = q_ref[0, rows(i), (g // heads_per_slab) * hw:(g // heads_per_slab + 1) * hw]
        mine = lane_head == g % heads_per_slab
        s_scr[g % 2] = _dot(jnp.where(mine, slab, jnp.zeros_like(slab)), kt_ref[0, 0])

    def probs(g):
        m = jnp.max(s_scr[g % 2], axis=-1, keepdims=True)
        p_scr[g % 2] = jnp.exp2(s_scr[g % 2] - m).astype(BF16)

    def weighted(i, g):
        acc = _dot(p_scr[g % 2], vx_ref[0])
        o_ref[0, 0, rows(i), g * hd:(g + 1) * hd] = (acc[:, :hd] / acc[:, hd:]).astype(BF16)

    scores(0, 0)

    def body(i, carry):
        for g in range(group):
            probs(g)
            if g + 1 < group:
                scores(i, g + 1)
            else:
                scores(jnp.minimum(i + 1, nt - 1), 0)
            weighted(i, g)
        return carry

    lax.fori_loop(0, nt, body, 0, unroll=ATT_TILE_UNROLL)


def _attention(q, kt, vx):
    b, n, d = q.shape
    hd = ATT_HD
    group = d // hd // ATT_KV_HEADS
    tq = ATT_Q_TILE
    assert group % 2 == 0
    return pl.pallas_call(
        functools.partial(_attention_kernel, group=group, tq=tq),
        grid=(b, ATT_KV_HEADS),
        in_specs=[
            pl.BlockSpec((1, n, group * hd), lambda bi, g: (bi, 0, g)),
            pl.BlockSpec((1, 1, 2 * hd, n), lambda bi, g: (bi, g, 0, 0)),
            pl.BlockSpec((1, n, 2 * hd), lambda bi, g: (bi, 0, g)),
        ],
        out_specs=pl.BlockSpec((1, 1, n, group * hd), lambda bi, g: (bi, g, 0, 0)),
        out_shape=jax.ShapeDtypeStruct((b, ATT_KV_HEADS, n, group * hd), BF16),
        scratch_shapes=[pltpu.VMEM((2, tq, n), F32), pltpu.VMEM((2, tq, n), BF16)],
        compiler_params=_params("arbitrary", "arbitrary"),
        name="gqa_attention",
    )(q, kt, vx)


def _rope_tables(n, head_dim, copies):
    rows = n // GRID_W
    nf = head_dim // 4
    inv = ROPE_THETA ** (-jnp.arange(nf, dtype=F32) / nf)
    ang_r = jnp.arange(rows, dtype=F32)[:, None] * inv[None, :]
    ang_c = jnp.arange(GRID_W, dtype=F32)[:, None] * inv[None, :]

    def table(fn):
        per_row = jnp.repeat(fn(ang_r), GRID_W, axis=0)
        per_col = jnp.tile(fn(ang_c), (rows, 1))
        return jnp.concatenate([per_row, per_col] * copies, axis=-1)

    return table(jnp.cos), table(jnp.sin)


def _trunk(x, mod_table, first_row, p):
    mod = ModRows(mod_table, 0, first_row)
    q, kt, v, gate = _ret_in(x, mod, p["norm1_g"][0], p["ret_w_in"], p["ret_cos"], p["ret_sin"])
    y = _retention(q, kt, v, p["ret_dec"])
    x = _out_mlp(x, y, gate, mod, p["ret_w_out"], p["norm2_g"][0], p["mlp_w1"][0], p["mlp_w2"][0],
                 p["final_g"], final=False)
    mod = ModRows(mod_table, 1, first_row)
    q, kt, vx = _att_in(x, mod, p["norm1_g"][1], p["att_w_in"], p["att_q_gain_a"], p["att_q_gain_b"],
                        p["att_k_gain_a"], p["att_k_gain_b"], p["att_cos"], p["att_sin"])
    y = _attention(q, kt, vx)
    return _out_mlp(x, y, None, mod, p["att_w_out"], p["norm2_g"][1], p["mlp_w1"][1], p["mlp_w2"][1],
                    p["final_g"], final=True)


def kernel(x_prompt, x_sample, c_prompt, c_sample, mod_w, mod_b, norm1_g, norm2_g, ret_w_in, ret_decay,
           ret_w_out, att_w_in, att_q_gain, att_k_gain, att_w_out, mlp_w1, mlp_w2, final_g):
    depth, d, _ = mod_w.shape
    assert depth == 2 and ret_w_in.shape[0] == 1 and att_w_in.shape[0] == 1
    bp, n, _ = x_prompt.shape
    assert x_sample.shape[1] == n

    c_all = jnp.concatenate([c_prompt, c_sample], axis=0)
    pad = (-c_all.shape[0]) % 16
    c_all = jnp.pad(c_all, ((0, pad), (0, 0)))
    mod = _adaln(c_all, mod_w, mod_b).reshape(depth, -1, N_MOD, d)

    dk = d // RET_HEADS
    dv = 2 * d // RET_HEADS
    w_ret = ret_w_in[0]
    ret_qk = w_ret[:, :2 * d].reshape(d, 2 * RET_HEADS, 2, 2, dk // 4).transpose(0, 1, 3, 2, 4)
    ret_w = jnp.concatenate([ret_qk.reshape(d, 2 * d), w_ret[:, 2 * d:]], axis=1)
    n_qk = d // ATT_HD + ATT_KV_HEADS
    w_att = att_w_in[0]
    att_qk = w_att[:, :n_qk * ATT_HD].reshape(d, n_qk // 2, 2, 2, 2, ATT_HD // 4).transpose(0, 1, 4, 2, 3, 5)
    att_w = jnp.concatenate([att_qk.reshape(d, n_qk * ATT_HD), w_att[:, n_qk * ATT_HD:]], axis=1)

    def slab_gain(gain, half):
        part = gain.reshape(2, 2, ATT_HD // 4)[:, half, :].reshape(-1)
        return jnp.tile(part, 2).reshape(1, ATT_HD)

    ret_cos, ret_sin = _rope_tables(n, dk, 1)
    att_cos, att_sin = _rope_tables(n, ATT_HD, 2)
    p = {
        "norm1_g": norm1_g.reshape(depth, 1, d),
        "norm2_g": norm2_g.reshape(depth, 1, d),
        "final_g": final_g.reshape(1, d),
        "ret_w_in": ret_w.astype(BF16),
        "ret_w_out": ret_w_out[0].astype(BF16),
        "att_w_in": att_w.astype(BF16),
        "att_w_out": att_w_out[0].astype(BF16),
        "att_q_gain_a": slab_gain(att_q_gain[0], 0),
        "att_q_gain_b": slab_gain(att_q_gain[0], 1),
        "att_k_gain_a": slab_gain(att_k_gain[0], 0),
        "att_k_gain_b": slab_gain(att_k_gain[0], 1),
        "mlp_w1": mlp_w1.astype(BF16),
        "mlp_w2": mlp_w2.astype(BF16),
        "ret_cos": ret_cos,
        "ret_sin": ret_sin,
        "att_cos": att_cos,
        "att_sin": att_sin,
        "ret_dec": jnp.pad(jnp.broadcast_to(ret_decay[0].T[:, :, None], (RET_HEADS, 2, dv)),
                           ((0, 0), (0, 6), (0, 0))),
    }
    y_prompt = _trunk(x_prompt, mod, 0, p)
    y_sample = _trunk(x_sample, mod, bp, p)
    return (y_prompt, y_sample)
```

```python
import functools
from typing import NamedTuple

import jax
import jax.numpy as jnp
from jax import lax
from jax.experimental import pallas as pl
from jax.experimental.pallas import tpu as pltpu

F32 = jnp.float32
BF16 = jnp.bfloat16

GRID_W = 64
ROPE_THETA = 10000.0
EPS = 1e-6
RET_HEADS = 4
ATT_HD = 128
ATT_KV_HEADS = 2
N_MOD = 6

V7X_VMEM_BYTES = 64 * 1024 * 1024
V7X_LANES = 128
V7X_MXU_DIM = 256

VMEM_LIMIT = V7X_VMEM_BYTES - 8 * 1024 * 1024
RET_TOK_TILE = 1024
SUB_TILE = 512
ATT_TOK_TILE = 1024
MLP_TOK_TILE = 1024
FF_CHUNK = 1024
RET_CHUNK = V7X_MXU_DIM
SCAN_UNROLL = 5
OUT_UNROLL = 8
ATT_Q_TILE = 256
ATT_TILE_UNROLL = 2
LOG2_E = 1.4426950408889634


def _params(*sem):
    return pltpu.CompilerParams(dimension_semantics=sem, vmem_limit_bytes=VMEM_LIMIT)


def _resident(shape):
    nd = len(shape)
    return pl.BlockSpec(shape, lambda *_: (0,) * nd, pipeline_mode=pl.Buffered(1))


def _rms(x):
    return x * lax.rsqrt(jnp.mean(x * x, axis=-1, keepdims=True) + EPS)


def _dot(a, b):
    return jnp.dot(a, b, preferred_element_type=F32)


def _adaln_kernel(c_ref, w_ref, b_ref, o_ref):
    c = c_ref[...]
    s = c * jax.nn.sigmoid(c)
    w = w_ref[0]
    s_hi = s.astype(BF16)
    s_lo = (s - s_hi.astype(F32)).astype(BF16)
    w_hi = w.astype(BF16)
    w_lo = (w - w_hi.astype(F32)).astype(BF16)
    o_ref[0] = _dot(s_hi, w_hi) + _dot(s_lo, w_hi) + _dot(s_hi, w_lo) + b_ref[0]


def _adaln(c_all, mod_w, mod_b):
    depth, d, n6 = mod_w.shape
    rows = c_all.shape[0]
    tn = n6 // 4
    return pl.pallas_call(
        _adaln_kernel,
        grid=(depth, n6 // tn),
        in_specs=[
            pl.BlockSpec((rows, d), lambda i, j: (0, 0)),
            pl.BlockSpec((1, d, tn), lambda i, j: (i, 0, j)),
            pl.BlockSpec((1, 1, tn), lambda i, j: (i, 0, j)),
        ],
        out_specs=pl.BlockSpec((1, rows, tn), lambda i, j: (i, 0, j)),
        out_shape=jax.ShapeDtypeStruct((depth, rows, n6), F32),
        compiler_params=_params("arbitrary", "arbitrary"),
        name="adaln_mod",
    )(c_all, mod_w, mod_b.reshape(depth, 1, n6))


class ModRows(NamedTuple):
    table: jax.Array
    layer: int
    first_row: int


def _mod_spec(mod):
    _, _, n_mod, d = mod.table.shape
    return pl.BlockSpec((1, 1, n_mod, d), lambda bi, i: (mod.layer, mod.first_row + bi, 0, 0))


def _modulated_norm(x, gain, shift, scale):
    return (_rms(x) * gain * (1.0 + scale) + shift).astype(BF16)


def _ret_in_kernel(x_ref, mod_ref, g_ref, w_ref, cos_ref, sin_ref,
                   q_ref, kt_ref, v_ref, gate_ref, h_scr, *, d):
    m = mod_ref[0, 0]
    hw = V7X_MXU_DIM
    half = V7X_LANES
    c = RET_CHUNK
    ts = h_scr.shape[1]
    n_sub = h_scr.shape[0]
    v_off = 2 * d
    g_off = 4 * d

    def normalise(sub):
        rs = slice(sub * ts, (sub + 1) * ts)
        h_scr[sub] = _modulated_norm(x_ref[0, rs, :], g_ref[...], m[0:1], m[1:2])

    def project(sub):
        rs = slice(sub * ts, (sub + 1) * ts)
        cos = cos_ref[rs, :]
        sin = sin_ref[rs, :]
        for hh in list(range(RET_HEADS, 2 * RET_HEADS)) + list(range(RET_HEADS)):
            acc = _dot(h_scr[sub], w_ref[:, hh * hw:(hh + 1) * hw])
            if hh < RET_HEADS:
                acc = acc * (hw ** -0.5)
            a = acc[:, :half]
            b = acc[:, half:]
            ra = a * cos - b * sin
            rb = a * sin + b * cos
            if hh < RET_HEADS:
                q_ref[0, hh, rs, :half] = ra.astype(BF16)
                q_ref[0, hh, rs, half:] = rb.astype(BF16)
            else:
                for ci in range(ts // c):
                    kc = sub * (ts // c) + ci
                    kt_ref[0, hh - RET_HEADS, kc, :half, :] = ra[ci * c:(ci + 1) * c].T.astype(BF16)
                    kt_ref[0, hh - RET_HEADS, kc, half:, :] = rb[ci * c:(ci + 1) * c].T.astype(BF16)
        dv = v_ref.shape[-1]
        for hh in range(RET_HEADS):
            g = _dot(h_scr[sub], w_ref[:, g_off + hh * dv:g_off + (hh + 1) * dv])
            gate_ref[0, hh, rs, :] = (g * jax.nn.sigmoid(g)).astype(BF16)
        for hh in range(RET_HEADS):
            v = _dot(h_scr[sub], w_ref[:, v_off + hh * dv:v_off + (hh + 1) * dv])
            v_ref[0, hh, rs, :] = v.astype(BF16)

    for sub in range(n_sub):
        normalise(sub)
    for sub in range(n_sub):
        project(sub)


def _ret_in(x, mod, gain, w, cos, sin):
    b, n, d = x.shape
    tm = RET_TOK_TILE
    c = RET_CHUNK
    dk = d // RET_HEADS
    dv = 2 * d // RET_HEADS
    return pl.pallas_call(
        functools.partial(_ret_in_kernel, d=d),
        grid=(b, n // tm),
        in_specs=[
            pl.BlockSpec((1, tm, d), lambda bi, i: (bi, i, 0)),
            _mod_spec(mod),
            _resident(gain.shape),
            _resident(w.shape),
            pl.BlockSpec((tm, V7X_LANES), lambda bi, i: (i, 0)),
            pl.BlockSpec((tm, V7X_LANES), lambda bi, i: (i, 0)),
        ],
        out_specs=[
            pl.BlockSpec((1, RET_HEADS, tm, dk), lambda bi, i: (bi, 0, i, 0)),
            pl.BlockSpec((1, RET_HEADS, tm // c, dk, c), lambda bi, i: (bi, 0, i, 0, 0)),
            pl.BlockSpec((1, RET_HEADS, tm, dv), lambda bi, i: (bi, 0, i, 0)),
            pl.BlockSpec((1, RET_HEADS, tm, dv), lambda bi, i: (bi, 0, i, 0)),
        ],
        out_shape=[
            jax.ShapeDtypeStruct((b, RET_HEADS, n, dk), BF16),
            jax.ShapeDtypeStruct((b, RET_HEADS, n // c, dk, c), BF16),
            jax.ShapeDtypeStruct((b, RET_HEADS, n, dv), BF16),
            jax.ShapeDtypeStruct((b, RET_HEADS, n, dv), BF16),
        ],
        scratch_shapes=[pltpu.VMEM((tm // SUB_TILE, SUB_TILE, d), BF16)],
        compiler_params=_params("arbitrary", "arbitrary"),
        name="ret_in_proj",
    )(x, mod.table, gain, w, cos, sin)


def _retention_kernel(q_ref, kt_ref, v_ref, dec_ref, o_ref,
                      sf_all, sb_all, sf_scr, sb_scr, qdf_scr, qdb_scr, kdf_scr, kdb_scr, dm_scr, *, nc):
    c = RET_CHUNK
    dk = q_ref.shape[-1]
    dv = v_ref.shape[-1]
    dec = dec_ref[0]
    neg = -dec
    log_g = -(jnp.maximum(neg, 0.0) + jnp.log1p(jnp.exp(-jnp.abs(neg))))
    lf = log_g[0:1, :]
    lb = log_g[1:2, :]
    qpos = lax.broadcasted_iota(jnp.int32, (c, dk), 0).astype(F32)
    qdf_scr[...] = jnp.exp(lf[:, :dk] * (qpos + 1.0))
    qdb_scr[...] = jnp.exp(lb[:, :dk] * (c - qpos))
    kpos = lax.broadcasted_iota(jnp.int32, (dk, c), 1).astype(F32)
    kdf_scr[...] = jnp.exp(lf[:, :c] * (c - 1.0 - kpos))
    kdb_scr[...] = jnp.exp(lb[:, :c] * kpos)
    diff = (lax.broadcasted_iota(jnp.int32, (c, c), 0)
            - lax.broadcasted_iota(jnp.int32, (c, c), 1)).astype(F32)
    dm_scr[...] = jnp.where(diff >= 0.0,
                            jnp.exp(lf[:, :c] * jnp.maximum(diff, 0.0)),
                            jnp.exp(lb[:, :c] * jnp.maximum(-diff, 0.0)))
    cdf = jnp.exp(lf[:, :dv] * c)
    cdb = jnp.exp(lb[:, :dv] * c)

    def rows(i):
        return pl.ds(pl.multiple_of(i * c, c), c)

    sf_scr[...] = jnp.zeros_like(sf_scr)
    sb_scr[...] = jnp.zeros_like(sb_scr)
    sf_all[0] = jnp.zeros((dk, dv), BF16)
    sb_all[nc - 1] = jnp.zeros((dk, dv), BF16)

    def scan_body(t, carry):
        kf = (kt_ref[0, 0, t].astype(F32) * kdf_scr[...]).astype(BF16)
        sf = sf_scr[...] * cdf + _dot(kf, v_ref[0, 0, rows(t), :])
        sf_scr[...] = sf
        sf_all[t + 1] = sf.astype(BF16)
        j = nc - 1 - t
        kb = (kt_ref[0, 0, j].astype(F32) * kdb_scr[...]).astype(BF16)
        sb = sb_scr[...] * cdb + _dot(kb, v_ref[0, 0, rows(j), :])
        sb_scr[...] = sb
        sb_all[j - 1] = sb.astype(BF16)
        return carry

    lax.fori_loop(0, nc - 1, scan_body, 0, unroll=SCAN_UNROLL)

    def out_body(i, carry):
        r = rows(i)
        q = q_ref[0, 0, r, :]
        v = v_ref[0, 0, r, :]
        p = (_dot(q, kt_ref[0, 0, i]) * dm_scr[...]).astype(BF16)
        q32 = q.astype(F32)
        qf = (q32 * qdf_scr[...]).astype(BF16)
        qb = (q32 * qdb_scr[...]).astype(BF16)
        y = _dot(p, v) + _dot(qf, sf_all[i]) + _dot(qb, sb_all[i])
        mu = jnp.mean(y, axis=-1, keepdims=True)
        yc = y - mu
        var = jnp.mean(yc * yc, axis=-1, keepdims=True)
        yn = yc * lax.rsqrt(var + EPS)
        o_ref[0, 0, r, :] = yn.astype(BF16)
        return carry

    lax.fori_loop(0, nc, out_body, 0, unroll=OUT_UNROLL)


def _retention(q, kt, v, dec):
    b, _, n, dk = q.shape
    dv = v.shape[-1]
    c = RET_CHUNK
    nc = n // c
    return pl.pallas_call(
        functools.partial(_retention_kernel, nc=nc),
        grid=(b, RET_HEADS),
        in_specs=[
            pl.BlockSpec((1, 1, n, dk), lambda bi, h: (bi, h, 0, 0)),
            pl.BlockSpec((1, 1, nc, dk, c), lambda bi, h: (bi, h, 0, 0, 0)),
            pl.BlockSpec((1, 1, n, dv), lambda bi, h: (bi, h, 0, 0)),
            pl.BlockSpec((1, 8, dv), lambda bi, h: (h, 0, 0)),
        ],
        out_specs=pl.BlockSpec((1, 1, n, dv), lambda bi, h: (bi, h, 0, 0)),
        out_shape=jax.ShapeDtypeStruct((b, RET_HEADS, n, dv), BF16),
        scratch_shapes=[
            pltpu.VMEM((nc, dk, dv), BF16),
            pltpu.VMEM((nc, dk, dv), BF16),
            pltpu.VMEM((dk, dv), F32),
            pltpu.VMEM((dk, dv), F32),
            pltpu.VMEM((c, dk), F32),
            pltpu.VMEM((c, dk), F32),
            pltpu.VMEM((dk, c), F32),
            pltpu.VMEM((dk, c), F32),
            pltpu.VMEM((c, c), F32),
        ],
        compiler_params=_params("arbitrary", "arbitrary"),
        name="retention_core",
    )(q, kt, v, dec)


def _out_mlp_kernel(x_ref, y_ref, gate_ref, mod_ref, wo_ref, g2_ref, w1_ref, w2_ref, fg_ref, o_ref,
                    h_scr, *, final, gated):
    m = mod_ref[0, 0]
    ts = h_scr.shape[1]
    n_sub = h_scr.shape[0]
    d_ff = w1_ref.shape[1]

    def mix(sub):
        rs = slice(sub * ts, (sub + 1) * ts)
        dy = y_ref.shape[-1]
        mixed = None
        for hy in range(y_ref.shape[1]):
            yh = y_ref[0, hy, rs, :]
            if gated:
                yh = yh * gate_ref[0, hy, rs, :]
            part = _dot(yh, wo_ref[hy * dy:(hy + 1) * dy, :])
            mixed = part if mixed is None else mixed + part
        x1 = x_ref[0, rs, :] + m[2:3] * mixed
        o_ref[0, rs, :] = x1
        h_scr[sub] = _modulated_norm(x1, g2_ref[...], m[3:4], m[4:5])

    def mlp(sub):
        rs = slice(sub * ts, (sub + 1) * ts)
        for cc in range(d_ff // FF_CHUNK):
            cs = slice(cc * FF_CHUNK, (cc + 1) * FF_CHUNK)
            u = jnp.maximum(_dot(h_scr[sub], w1_ref[:, cs]), 0.0)
            o_ref[0, rs, :] += m[5:6] * _dot((u * u).astype(BF16), w2_ref[cs, :])
        if final:
            o_ref[0, rs, :] = _rms(o_ref[0, rs, :]) * fg_ref[...]

    for sub in range(n_sub):
        mix(sub)
    for sub in range(n_sub):
        mlp(sub)


def _out_mlp(x, y, gate, mod, w_out, gain2, w1, w2, final_g, final):
    b, n, d = x.shape
    tm = MLP_TOK_TILE
    _, hy, _, dy = y.shape
    y_spec = pl.BlockSpec((1, hy, tm, dy), lambda bi, i: (bi, 0, i, 0))
    gated = gate is not None
    if not gated:
        gate = jnp.zeros((1, 1, 8, V7X_LANES), BF16)
    gate_spec = y_spec if gated else _resident(gate.shape)
    return pl.pallas_call(
        functools.partial(_out_mlp_kernel, final=final, gated=gated),
        grid=(b, n // tm),
        in_specs=[
            pl.BlockSpec((1, tm, d), lambda bi, i: (bi, i, 0)),
            y_spec,
            gate_spec,
            _mod_spec(mod),
            _resident(w_out.shape),
            _resident(gain2.shape),
            _resident(w1.shape),
            _resident(w2.shape),
            _resident(final_g.shape),
        ],
        out_specs=pl.BlockSpec((1, tm, d), lambda bi, i: (bi, i, 0)),
        out_shape=jax.ShapeDtypeStruct((b, n, d), F32),
        scratch_shapes=[pltpu.VMEM((tm // SUB_TILE, SUB_TILE, d), BF16)],
        compiler_params=_params("arbitrary", "arbitrary"),
        name="out_proj_mlp_final" if final else "out_proj_mlp",
    )(x, y, gate, mod.table, w_out, gain2, w1, w2, final_g)


def _att_in_kernel(x_ref, mod_ref, g_ref, w_ref, qga_ref, qgb_ref, kga_ref, kgb_ref, cos_ref, sin_ref,
                   qt_ref, kx_ref, vt_ref, h_scr, *, d):
    m = mod_ref[0, 0]
    h_scr[...] = _modulated_norm(x_ref[0], g_ref[...], m[0:1], m[1:2])
    hd = ATT_HD
    hw = V7X_MXU_DIM
    half = hw // 2
    part = hd // 2
    n_q_slabs = d // hw
    cos = cos_ref[...]
    sin = sin_ref[...]

    def tables(ga, gb):
        return cos * ga, sin * gb, sin * ga, cos * gb

    q_scale = hd ** -0.5 * LOG2_E
    q_tab = tables(qga_ref[...] * q_scale, qgb_ref[...] * q_scale)
    k_tab = tables(kga_ref[...], kgb_ref[...])
    first = lax.broadcasted_iota(jnp.int32, (1, half), 1) < part
    tq = qt_ref.shape[-1]
    for sl in range(n_q_slabs + 1):
        acc = _dot(h_scr[...], w_ref[:, sl * hw:(sl + 1) * hw])
        x = acc[:, :half]
        y = acc[:, half:]
        u = x * x + y * y
        ss0 = jnp.sum(jnp.where(first, u, 0.0), axis=-1, keepdims=True)
        ss1 = jnp.sum(jnp.where(first, 0.0, u), axis=-1, keepdims=True)
        r = lax.rsqrt(jnp.where(first, ss0, ss1) * (1.0 / hd) + EPS)
        c_a, s_b, s_a, c_b = q_tab if sl < n_q_slabs else k_tab
        ox = (x * c_a - y * s_b) * r
        oy = (x * s_a + y * c_b) * r
        if sl < n_q_slabs:
            for ti in range(x_ref.shape[1] // tq):
                qt_ref[0, sl, ti, :half, :] = ox[ti * tq:(ti + 1) * tq].T.astype(BF16)
                qt_ref[0, sl, ti, half:, :] = oy[ti * tq:(ti + 1) * tq].T.astype(BF16)
        else:
            kx_ref[0, :, :half] = ox.astype(BF16)
            kx_ref[0, :, half:] = oy.astype(BF16)
    v_off = (n_q_slabs + 1) * hw
    v = _dot(h_scr[...], w_ref[:, v_off:v_off + ATT_KV_HEADS * hd])
    for g in range(ATT_KV_HEADS):
        vt_ref[0, g] = v[:, g * hd:(g + 1) * hd].T.astype(BF16)


def _att_in(x, mod, gain, w, q_gain_a, q_gain_b, k_gain_a, k_gain_b, cos, sin):
    b, n, d = x.shape
    tm = ATT_TOK_TILE
    tq = ATT_Q_TILE
    hd = ATT_HD
    hw = V7X_MXU_DIM
    assert ATT_KV_HEADS * hd == hw
    return pl.pallas_call(
        functools.partial(_att_in_kernel, d=d),
        grid=(b, n // tm),
        in_specs=[
            pl.BlockSpec((1, tm, d), lambda bi, i: (bi, i, 0)),
            _mod_spec(mod),
            _resident(gain.shape),
            _resident(w.shape),
            _resident(q_gain_a.shape),
            _resident(q_gain_b.shape),
            _resident(k_gain_a.shape),
            _resident(k_gain_b.shape),
            pl.BlockSpec((tm, hd), lambda bi, i: (i, 0)),
            pl.BlockSpec((tm, hd), lambda bi, i: (i, 0)),
        ],
        out_specs=[
            pl.BlockSpec((1, d // hw, tm // tq, hw, tq), lambda bi, i: (bi, 0, i, 0, 0)),
            pl.BlockSpec((1, tm, hw), lambda bi, i: (bi, i, 0)),
            pl.BlockSpec((1, ATT_KV_HEADS, hd, tm), lambda bi, i: (bi, 0, 0, i)),
        ],
        out_shape=[
            jax.ShapeDtypeStruct((b, d // hw, n // tq, hw, tq), BF16),
            jax.ShapeDtypeStruct((b, n, hw), BF16),
            jax.ShapeDtypeStruct((b, ATT_KV_HEADS, hd, n), BF16),
        ],
        scratch_shapes=[pltpu.VMEM((tm, d), BF16)],
        compiler_params=_params("arbitrary", "arbitrary"),
        name="att_in_proj",
    )(x, mod.table, gain, w, q_gain_a, q_gain_b, k_gain_a, k_gain_b, cos, sin)


def _attention_kernel(qt_ref, kx_ref, vt_ref, o_ref, s_scr, p_scr, l_scr, *, group, tq):
    hd = ATT_HD
    nt = qt_ref.shape[2]
    hw = qt_ref.shape[3]
    row_group = (lax.broadcasted_iota(jnp.int32, (hw, 1), 0) // (hd // 2)) % ATT_KV_HEADS
    mine = row_group == pl.program_id(1)

    def rows(i):
        return pl.ds(pl.multiple_of(i * tq, tq), tq)

    def scores(i, g):
        slab = qt_ref[0, g, i]
        s_scr[g % 2] = _dot(kx_ref[0], jnp.where(mine, slab, jnp.zeros_like(slab)))

    def probs(g):
        m = jnp.max(s_scr[g % 2], axis=0, keepdims=True)
        e = jnp.exp2(s_scr[g % 2] - m)
        l_scr[g % 2] = jnp.sum(e, axis=0, keepdims=True)
        p_scr[g % 2] = e.astype(BF16)

    def weighted(i, g):
        out_t = _dot(vt_ref[0, 0], p_scr[g % 2]) / l_scr[g % 2]
        o_ref[0, 0, rows(i), g * hd:(g + 1) * hd] = out_t.T.astype(BF16)

    scores(0, 0)

    def body(i, carry):
        for g in range(group):
            probs(g)
            if g + 1 < group:
                scores(i, g + 1)
            else:
                scores(jnp.minimum(i + 1, nt - 1), 0)
            weighted(i, g)
        return carry

    lax.fori_loop(0, nt, body, 0, unroll=ATT_TILE_UNROLL)


def _attention(qt, kx, vt):
    b, group, nt, hw, tq = qt.shape
    n = kx.shape[1]
    hd = ATT_HD
    assert group % 2 == 0
    return pl.pallas_call(
        functools.partial(_attention_kernel, group=group, tq=tq),
        grid=(b, ATT_KV_HEADS),
        in_specs=[
            pl.BlockSpec((1, group, nt, hw, tq), lambda bi, g: (bi, 0, 0, 0, 0)),
            pl.BlockSpec((1, n, hw), lambda bi, g: (bi, 0, 0)),
            pl.BlockSpec((1, 1, hd, n), lambda bi, g: (bi, g, 0, 0)),
        ],
        out_specs=pl.BlockSpec((1, 1, n, group * hd), lambda bi, g: (bi, g, 0, 0)),
        out_shape=jax.ShapeDtypeStruct((b, ATT_KV_HEADS, n, group * hd), BF16),
        scratch_shapes=[pltpu.VMEM((2, n, tq), F32), pltpu.VMEM((2, n, tq), BF16),
                        pltpu.VMEM((2, 1, tq), F32)],
        compiler_params=_params("arbitrary", "arbitrary"),
        name="gqa_attention",
    )(qt, kx, vt)


def _rope_tables(n, head_dim, copies):
    rows = n // GRID_W
    nf = head_dim // 4
    inv = ROPE_THETA ** (-jnp.arange(nf, dtype=F32) / nf)
    ang_r = jnp.arange(rows, dtype=F32)[:, None] * inv[None, :]
    ang_c = jnp.arange(GRID_W, dtype=F32)[:, None] * inv[None, :]

    def table(fn):
        per_row = jnp.repeat(fn(ang_r), GRID_W, axis=0)
        per_col = jnp.tile(fn(ang_c), (rows, 1))
        return jnp.concatenate([per_row, per_col] * copies, axis=-1)

    return table(jnp.cos), table(jnp.sin)


def _trunk(x, mod_table, first_row, p):
    mod = ModRows(mod_table, 0, first_row)
    q, kt, v, gate = _ret_in(x, mod, p["norm1_g"][0], p["ret_w_in"], p["ret_cos"], p["ret_sin"])
    y = _retention(q, kt, v, p["ret_dec"])
    x = _out_mlp(x, y, gate, mod, p["ret_w_out"], p["norm2_g"][0], p["mlp_w1"][0], p["mlp_w2"][0],
                 p["final_g"], final=False)
    mod = ModRows(mod_table, 1, first_row)
    qt, kx, vt = _att_in(x, mod, p["norm1_g"][1], p["att_w_in"], p["att_q_gain_a"], p["att_q_gain_b"],
                         p["att_k_gain_a"], p["att_k_gain_b"], p["att_cos"], p["att_sin"])
    y = _attention(qt, kx, vt)
    return _out_mlp(x, y, None, mod, p["att_w_out"], p["norm2_g"][1], p["mlp_w1"][1], p["mlp_w2"][1],
                    p["final_g"], final=True)


def kernel(x_prompt, x_sample, c_prompt, c_sample, mod_w, mod_b, norm1_g, norm2_g, ret_w_in, ret_decay,
           ret_w_out, att_w_in, att_q_gain, att_k_gain, att_w_out, mlp_w1, mlp_w2, final_g):
    depth, d, _ = mod_w.shape
    assert depth == 2 and ret_w_in.shape[0] == 1 and att_w_in.shape[0] == 1
    bp, n, _ = x_prompt.shape
    assert x_sample.shape[1] == n

    c_all = jnp.concatenate([c_prompt, c_sample], axis=0)
    pad = (-c_all.shape[0]) % 16
    c_all = jnp.pad(c_all, ((0, pad), (0, 0)))
    mod = _adaln(c_all, mod_w, mod_b).reshape(depth, -1, N_MOD, d)

    dk = d // RET_HEADS
    dv = 2 * d // RET_HEADS
    w_ret = ret_w_in[0]
    ret_qk = w_ret[:, :2 * d].reshape(d, 2 * RET_HEADS, 2, 2, dk // 4).transpose(0, 1, 3, 2, 4)
    ret_w = jnp.concatenate([ret_qk.reshape(d, 2 * d), w_ret[:, 2 * d:]], axis=1)
    w_att = att_w_in[0]
    group = d // ATT_HD // ATT_KV_HEADS
    att_q = w_att[:, :d].reshape(d, ATT_KV_HEADS, group, 2, 2, ATT_HD // 4).transpose(0, 2, 4, 1, 3, 5)
    att_k = w_att[:, d:d + ATT_KV_HEADS * ATT_HD].reshape(d, ATT_KV_HEADS, 2, 2, ATT_HD // 4)
    att_k = att_k.transpose(0, 3, 1, 2, 4)
    att_w = jnp.concatenate([att_q.reshape(d, d), att_k.reshape(d, ATT_KV_HEADS * ATT_HD),
                             w_att[:, d + ATT_KV_HEADS * ATT_HD:]], axis=1)

    def slab_gain(gain, half):
        part = gain.reshape(2, 2, ATT_HD // 4)[:, half, :].reshape(-1)
        return jnp.tile(part, 2).reshape(1, ATT_HD)

    ret_cos, ret_sin = _rope_tables(n, dk, 1)
    att_cos, att_sin = _rope_tables(n, ATT_HD, 2)
    p = {
        "norm1_g": norm1_g.reshape(depth, 1, d),
        "norm2_g": norm2_g.reshape(depth, 1, d),
        "final_g": final_g.reshape(1, d),
        "ret_w_in": ret_w.astype(BF16),
        "ret_w_out": ret_w_out[0].astype(BF16),
        "att_w_in": att_w.astype(BF16),
        "att_w_out": att_w_out[0].astype(BF16),
        "att_q_gain_a": slab_gain(att_q_gain[0], 0),
        "att_q_gain_b": slab_gain(att_q_gain[0], 1),
        "att_k_gain_a": slab_gain(att_k_gain[0], 0),
        "att_k_gain_b": slab_gain(att_k_gain[0], 1),
        "mlp_w1": mlp_w1.astype(BF16),
        "mlp_w2": mlp_w2.astype(BF16),
        "ret_cos": ret_cos,
        "ret_sin": ret_sin,
        "att_cos": att_cos,
        "att_sin": att_sin,
        "ret_dec": jnp.pad(jnp.broadcast_to(ret_decay[0].T[:, :, None], (RET_HEADS, 2, dv)),
                           ((0, 0), (0, 6), (0, 0))),
    }
    y_prompt = _trunk(x_prompt, mod, 0, p)
    y_sample = _trunk(x_sample, mod, bp, p)
    return (y_prompt, y_sample)
```

```python
import functools
from typing import NamedTuple

import jax
import jax.numpy as jnp
from jax import lax
from jax.experimental import pallas as pl
from jax.experimental.pallas import tpu as pltpu

F32 = jnp.float32
BF16 = jnp.bfloat16

GRID_W = 64
ROPE_THETA = 10000.0
EPS = 1e-6
RET_HEADS = 4
ATT_HD = 128
ATT_KV_HEADS = 2
N_MOD = 6

V7X_VMEM_BYTES = 64 * 1024 * 1024
V7X_LANES = 128
V7X_MXU_DIM = 256

VMEM_LIMIT = V7X_VMEM_BYTES - 8 * 1024 * 1024
RET_TOK_TILE = 1024
SUB_TILE = 512
ATT_TOK_TILE = 1024
MLP_TOK_TILE = 1024
FF_CHUNK = 1024
RET_CHUNK = V7X_MXU_DIM
SCAN_UNROLL = 5
OUT_UNROLL = 8
ATT_Q_TILE = 256
ATT_TILE_UNROLL = 4
ONES_ROWS = 16
LOG2_E = 1.4426950408889634


def _params(*sem):
    return pltpu.CompilerParams(dimension_semantics=sem, vmem_limit_bytes=VMEM_LIMIT)


def _resident(shape):
    nd = len(shape)
    return pl.BlockSpec(shape, lambda *_: (0,) * nd, pipeline_mode=pl.Buffered(1))


def _rms(x):
    return x * lax.rsqrt(jnp.mean(x * x, axis=-1, keepdims=True) + EPS)


def _dot(a, b):
    return jnp.dot(a, b, preferred_element_type=F32)


def _adaln_kernel(c_ref, w_ref, b_ref, o_ref):
    c = c_ref[...]
    s = c * jax.nn.sigmoid(c)
    w = w_ref[0]
    s_hi = s.astype(BF16)
    s_lo = (s - s_hi.astype(F32)).astype(BF16)
    w_hi = w.astype(BF16)
    w_lo = (w - w_hi.astype(F32)).astype(BF16)
    o_ref[0] = _dot(s_hi, w_hi) + _dot(s_lo, w_hi) + _dot(s_hi, w_lo) + b_ref[0]


def _adaln(c_all, mod_w, mod_b):
    depth, d, n6 = mod_w.shape
    rows = c_all.shape[0]
    tn = n6 // 4
    return pl.pallas_call(
        _adaln_kernel,
        grid=(depth, n6 // tn),
        in_specs=[
            pl.BlockSpec((rows, d), lambda i, j: (0, 0)),
            pl.BlockSpec((1, d, tn), lambda i, j: (i, 0, j)),
            pl.BlockSpec((1, 1, tn), lambda i, j: (i, 0, j)),
        ],
        out_specs=pl.BlockSpec((1, rows, tn), lambda i, j: (i, 0, j)),
        out_shape=jax.ShapeDtypeStruct((depth, rows, n6), F32),
        compiler_params=_params("arbitrary", "arbitrary"),
        name="adaln_mod",
    )(c_all, mod_w, mod_b.reshape(depth, 1, n6))


class ModRows(NamedTuple):
    table: jax.Array
    layer: int
    first_row: int


def _mod_spec(mod):
    _, _, n_mod, d = mod.table.shape
    return pl.BlockSpec((1, 1, n_mod, d), lambda bi, i: (mod.layer, mod.first_row + bi, 0, 0))


def _modulated_norm(x, gain, shift, scale):
    return (_rms(x) * gain * (1.0 + scale) + shift).astype(BF16)


def _ret_in_kernel(x_ref, mod_ref, g_ref, w_ref, cos_ref, sin_ref,
                   q_ref, kt_ref, v_ref, gate_ref, h_scr, *, d):
    m = mod_ref[0, 0]
    hw = V7X_MXU_DIM
    half = V7X_LANES
    c = RET_CHUNK
    ts = h_scr.shape[1]
    n_sub = h_scr.shape[0]
    v_off = 2 * d
    g_off = 4 * d

    def normalise(sub):
        rs = slice(sub * ts, (sub + 1) * ts)
        h_scr[sub] = _modulated_norm(x_ref[0, rs, :], g_ref[...], m[0:1], m[1:2])

    def project(sub):
        rs = slice(sub * ts, (sub + 1) * ts)
        cos = cos_ref[rs, :]
        sin = sin_ref[rs, :]
        for hh in list(range(RET_HEADS, 2 * RET_HEADS)) + list(range(RET_HEADS)):
            acc = _dot(h_scr[sub], w_ref[:, hh * hw:(hh + 1) * hw])
            if hh < RET_HEADS:
                acc = acc * (hw ** -0.5)
            a = acc[:, :half]
            b = acc[:, half:]
            ra = a * cos - b * sin
            rb = a * sin + b * cos
            if hh < RET_HEADS:
                q_ref[0, hh, rs, :half] = ra.astype(BF16)
                q_ref[0, hh, rs, half:] = rb.astype(BF16)
            else:
                for ci in range(ts // c):
                    kc = sub * (ts // c) + ci
                    kt_ref[0, hh - RET_HEADS, kc, :half, :] = ra[ci * c:(ci + 1) * c].T.astype(BF16)
                    kt_ref[0, hh - RET_HEADS, kc, half:, :] = rb[ci * c:(ci + 1) * c].T.astype(BF16)
        dv = v_ref.shape[-1]
        for hh in range(RET_HEADS):
            g = _dot(h_scr[sub], w_ref[:, g_off + hh * dv:g_off + (hh + 1) * dv])
            gate_ref[0, hh, rs, :] = (g * jax.nn.sigmoid(g)).astype(BF16)
        for hh in range(RET_HEADS):
            v = _dot(h_scr[sub], w_ref[:, v_off + hh * dv:v_off + (hh + 1) * dv])
            v_ref[0, hh, rs, :] = v.astype(BF16)

    for sub in range(n_sub):
        normalise(sub)
    for sub in range(n_sub):
        project(sub)


def _ret_in(x, mod, gain, w, cos, sin):
    b, n, d = x.shape
    tm = RET_TOK_TILE
    c = RET_CHUNK
    dk = d // RET_HEADS
    dv = 2 * d // RET_HEADS
    return pl.pallas_call(
        functools.partial(_ret_in_kernel, d=d),
        grid=(b, n // tm),
        in_specs=[
            pl.BlockSpec((1, tm, d), lambda bi, i: (bi, i, 0)),
            _mod_spec(mod),
            _resident(gain.shape),
            _resident(w.shape),
            pl.BlockSpec((tm, V7X_LANES), lambda bi, i: (i, 0)),
            pl.BlockSpec((tm, V7X_LANES), lambda bi, i: (i, 0)),
        ],
        out_specs=[
            pl.BlockSpec((1, RET_HEADS, tm, dk), lambda bi, i: (bi, 0, i, 0)),
            pl.BlockSpec((1, RET_HEADS, tm // c, dk, c), lambda bi, i: (bi, 0, i, 0, 0)),
            pl.BlockSpec((1, RET_HEADS, tm, dv), lambda bi, i: (bi, 0, i, 0)),
            pl.BlockSpec((1, RET_HEADS, tm, dv), lambda bi, i: (bi, 0, i, 0)),
        ],
        out_shape=[
            jax.ShapeDtypeStruct((b, RET_HEADS, n, dk), BF16),
            jax.ShapeDtypeStruct((b, RET_HEADS, n // c, dk, c), BF16),
            jax.ShapeDtypeStruct((b, RET_HEADS, n, dv), BF16),
            jax.ShapeDtypeStruct((b, RET_HEADS, n, dv), BF16),
        ],
        scratch_shapes=[pltpu.VMEM((tm // SUB_TILE, SUB_TILE, d), BF16)],
        compiler_params=_params("arbitrary", "arbitrary"),
        name="ret_in_proj",
    )(x, mod.table, gain, w, cos, sin)


def _retention_kernel(q_ref, kt_ref, v_ref, dec_ref, o_ref,
                      sf_all, sb_all, sf_scr, sb_scr, qdf_scr, qdb_scr, kdf_scr, kdb_scr, dm_scr, *, nc):
    c = RET_CHUNK
    dk = q_ref.shape[-1]
    dv = v_ref.shape[-1]
    dec = dec_ref[0]
    neg = -dec
    log_g = -(jnp.maximum(neg, 0.0) + jnp.log1p(jnp.exp(-jnp.abs(neg))))
    lf = log_g[0:1, :]
    lb = log_g[1:2, :]
    qpos = lax.broadcasted_iota(jnp.int32, (c, dk), 0).astype(F32)
    qdf_scr[...] = jnp.exp(lf[:, :dk] * (qpos + 1.0))
    qdb_scr[...] = jnp.exp(lb[:, :dk] * (c - qpos))
    kpos = lax.broadcasted_iota(jnp.int32, (dk, c), 1).astype(F32)
    kdf_scr[...] = jnp.exp(lf[:, :c] * (c - 1.0 - kpos))
    kdb_scr[...] = jnp.exp(lb[:, :c] * kpos)
    diff = (lax.broadcasted_iota(jnp.int32, (c, c), 0)
            - lax.broadcasted_iota(jnp.int32, (c, c), 1)).astype(F32)
    dm_scr[...] = jnp.where(diff >= 0.0,
                            jnp.exp(lf[:, :c] * jnp.maximum(diff, 0.0)),
                            jnp.exp(lb[:, :c] * jnp.maximum(-diff, 0.0)))
    cdf = jnp.exp(lf[:, :dv] * c)
    cdb = jnp.exp(lb[:, :dv] * c)

    def rows(i):
        return pl.ds(pl.multiple_of(i * c, c), c)

    sf_scr[...] = jnp.zeros_like(sf_scr)
    sb_scr[...] = jnp.zeros_like(sb_scr)
    sf_all[0] = jnp.zeros((dk, dv), BF16)
    sb_all[nc - 1] = jnp.zeros((dk, dv), BF16)

    def scan_body(t, carry):
        kf = (kt_ref[0, 0, t].astype(F32) * kdf_scr[...]).astype(BF16)
        sf = sf_scr[...] * cdf + _dot(kf, v_ref[0, 0, rows(t), :])
        sf_scr[...] = sf
        sf_all[t + 1] = sf.astype(BF16)
        j = nc - 1 - t
        kb = (kt_ref[0, 0, j].astype(F32) * kdb_scr[...]).astype(BF16)
        sb = sb_scr[...] * cdb + _dot(kb, v_ref[0, 0, rows(j), :])
        sb_scr[...] = sb
        sb_all[j - 1] = sb.astype(BF16)
        return carry

    lax.fori_loop(0, nc - 1, scan_body, 0, unroll=SCAN_UNROLL)

    def out_body(i, carry):
        r = rows(i)
        q = q_ref[0, 0, r, :]
        v = v_ref[0, 0, r, :]
        p = (_dot(q, kt_ref[0, 0, i]) * dm_scr[...]).astype(BF16)
        q32 = q.astype(F32)
        qf = (q32 * qdf_scr[...]).astype(BF16)
        qb = (q32 * qdb_scr[...]).astype(BF16)
        y = _dot(p, v) + _dot(qf, sf_all[i]) + _dot(qb, sb_all[i])
        mu = jnp.mean(y, axis=-1, keepdims=True)
        yc = y - mu
        var = jnp.mean(yc * yc, axis=-1, keepdims=True)
        yn = yc * lax.rsqrt(var + EPS)
        o_ref[0, 0, r, :] = yn.astype(BF16)
        return carry

    lax.fori_loop(0, nc, out_body, 0, unroll=OUT_UNROLL)


def _retention(q, kt, v, dec):
    b, _, n, dk = q.shape
    dv = v.shape[-1]
    c = RET_CHUNK
    nc = n // c
    return pl.pallas_call(
        functools.partial(_retention_kernel, nc=nc),
        grid=(b, RET_HEADS),
        in_specs=[
            pl.BlockSpec((1, 1, n, dk), lambda bi, h: (bi, h, 0, 0)),
            pl.BlockSpec((1, 1, nc, dk, c), lambda bi, h: (bi, h, 0, 0, 0)),
            pl.BlockSpec((1, 1, n, dv), lambda bi, h: (bi, h, 0, 0)),
            pl.BlockSpec((1, 8, dv), lambda bi, h: (h, 0, 0)),
        ],
        out_specs=pl.BlockSpec((1, 1, n, dv), lambda bi, h: (bi, h, 0, 0)),
        out_shape=jax.ShapeDtypeStruct((b, RET_HEADS, n, dv), BF16),
        scratch_shapes=[
            pltpu.VMEM((nc, dk, dv), BF16),
            pltpu.VMEM((nc, dk, dv), BF16),
            pltpu.VMEM((dk, dv), F32),
            pltpu.VMEM((dk, dv), F32),
            pltpu.VMEM((c, dk), F32),
            pltpu.VMEM((c, dk), F32),
            pltpu.VMEM((dk, c), F32),
            pltpu.VMEM((dk, c), F32),
            pltpu.VMEM((c, c), F32),
        ],
        compiler_params=_params("arbitrary", "arbitrary"),
        name="retention_core",
    )(q, kt, v, dec)


def _out_mlp_kernel(x_ref, y_ref, gate_ref, mod_ref, wo_ref, g2_ref, w1_ref, w2_ref, fg_ref, o_ref,
                    h_scr, *, final, gated):
    m = mod_ref[0, 0]
    ts = h_scr.shape[1]
    n_sub = h_scr.shape[0]
    d_ff = w1_ref.shape[1]

    def mix(sub):
        rs = slice(sub * ts, (sub + 1) * ts)
        dy = y_ref.shape[-1]
        mixed = None
        for hy in range(y_ref.shape[1]):
            yh = y_ref[0, hy, rs, :]
            if gated:
                yh = yh * gate_ref[0, hy, rs, :]
            part = _dot(yh, wo_ref[hy * dy:(hy + 1) * dy, :])
            mixed = part if mixed is None else mixed + part
        x1 = x_ref[0, rs, :] + m[2:3] * mixed
        o_ref[0, rs, :] = x1
        h_scr[sub] = _modulated_norm(x1, g2_ref[...], m[3:4], m[4:5])

    def mlp(sub):
        rs = slice(sub * ts, (sub + 1) * ts)
        for cc in range(d_ff // FF_CHUNK):
            cs = slice(cc * FF_CHUNK, (cc + 1) * FF_CHUNK)
            u = jnp.maximum(_dot(h_scr[sub], w1_ref[:, cs]), 0.0)
            o_ref[0, rs, :] += m[5:6] * _dot((u * u).astype(BF16), w2_ref[cs, :])
        if final:
            o_ref[0, rs, :] = _rms(o_ref[0, rs, :]) * fg_ref[...]

    for sub in range(n_sub):
        mix(sub)
    for sub in range(n_sub):
        mlp(sub)


def _out_mlp(x, y, gate, mod, w_out, gain2, w1, w2, final_g, final):
    b, n, d = x.shape
    tm = MLP_TOK_TILE
    _, hy, _, dy = y.shape
    y_spec = pl.BlockSpec((1, hy, tm, dy), lambda bi, i: (bi, 0, i, 0))
    gated = gate is not None
    if not gated:
        gate = jnp.zeros((1, 1, 8, V7X_LANES), BF16)
    gate_spec = y_spec if gated else _resident(gate.shape)
    return pl.pallas_call(
        functools.partial(_out_mlp_kernel, final=final, gated=gated),
        grid=(b, n // tm),
        in_specs=[
            pl.BlockSpec((1, tm, d), lambda bi, i: (bi, i, 0)),
            y_spec,
            gate_spec,
            _mod_spec(mod),
            _resident(w_out.shape),
            _resident(gain2.shape),
            _resident(w1.shape),
            _resident(w2.shape),
            _resident(final_g.shape),
        ],
        out_specs=pl.BlockSpec((1, tm, d), lambda bi, i: (bi, i, 0)),
        out_shape=jax.ShapeDtypeStruct((b, n, d), F32),
        scratch_shapes=[pltpu.VMEM((tm // SUB_TILE, SUB_TILE, d), BF16)],
        compiler_params=_params("arbitrary", "arbitrary"),
        name="out_proj_mlp_final" if final else "out_proj_mlp",
    )(x, y, gate, mod.table, w_out, gain2, w1, w2, final_g)


def _att_in_kernel(x_ref, mod_ref, g_ref, w_ref, qga_ref, qgb_ref, kga_ref, kgb_ref, cos_ref, sin_ref,
                   qt_ref, kx_ref, vt_ref, h_scr, *, d):
    m = mod_ref[0, 0]
    h_scr[...] = _modulated_norm(x_ref[0], g_ref[...], m[0:1], m[1:2])
    hd = ATT_HD
    hw = V7X_MXU_DIM
    half = hw // 2
    part = hd // 2
    n_q_slabs = d // hw
    cos = cos_ref[...]
    sin = sin_ref[...]

    def tables(ga, gb):
        return cos * ga, sin * gb, sin * ga, cos * gb

    q_scale = hd ** -0.5 * LOG2_E
    q_tab = tables(qga_ref[...] * q_scale, qgb_ref[...] * q_scale)
    k_tab = tables(kga_ref[...], kgb_ref[...])
    first = lax.broadcasted_iota(jnp.int32, (1, half), 1) < part
    tq = qt_ref.shape[-1]
    for sl in range(n_q_slabs + 1):
        acc = _dot(h_scr[...], w_ref[:, sl * hw:(sl + 1) * hw])
        x = acc[:, :half]
        y = acc[:, half:]
        u = x * x + y * y
        ss0 = jnp.sum(jnp.where(first, u, 0.0), axis=-1, keepdims=True)
        ss1 = jnp.sum(jnp.where(first, 0.0, u), axis=-1, keepdims=True)
        r = lax.rsqrt(jnp.where(first, ss0, ss1) * (1.0 / hd) + EPS)
        c_a, s_b, s_a, c_b = q_tab if sl < n_q_slabs else k_tab
        ox = (x * c_a - y * s_b) * r
        oy = (x * s_a + y * c_b) * r
        if sl < n_q_slabs:
            for ti in range(x_ref.shape[1] // tq):
                qt_ref[0, sl, ti, :half, :] = ox[ti * tq:(ti + 1) * tq].T.astype(BF16)
                qt_ref[0, sl, ti, half:, :] = oy[ti * tq:(ti + 1) * tq].T.astype(BF16)
        else:
            kx_ref[0, :, :half] = ox.astype(BF16)
            kx_ref[0, :, half:] = oy.astype(BF16)
    v_off = (n_q_slabs + 1) * hw
    v = _dot(h_scr[...], w_ref[:, v_off:v_off + ATT_KV_HEADS * hd])
    for g in range(ATT_KV_HEADS):
        vt_ref[0, g, :hd, :] = v[:, g * hd:(g + 1) * hd].T.astype(BF16)
        vt_ref[0, g, hd:, :] = jnp.ones((vt_ref.shape[2] - hd, x_ref.shape[1]), BF16)


def _att_in(x, mod, gain, w, q_gain_a, q_gain_b, k_gain_a, k_gain_b, cos, sin):
    b, n, d = x.shape
    tm = ATT_TOK_TILE
    tq = ATT_Q_TILE
    hd = ATT_HD
    hw = V7X_MXU_DIM
    assert ATT_KV_HEADS * hd == hw
    return pl.pallas_call(
        functools.partial(_att_in_kernel, d=d),
        grid=(b, n // tm),
        in_specs=[
            pl.BlockSpec((1, tm, d), lambda bi, i: (bi, i, 0)),
            _mod_spec(mod),
            _resident(gain.shape),
            _resident(w.shape),
            _resident(q_gain_a.shape),
            _resident(q_gain_b.shape),
            _resident(k_gain_a.shape),
            _resident(k_gain_b.shape),
            pl.BlockSpec((tm, hd), lambda bi, i: (i, 0)),
            pl.BlockSpec((tm, hd), lambda bi, i: (i, 0)),
        ],
        out_specs=[
            pl.BlockSpec((1, d // hw, tm // tq, hw, tq), lambda bi, i: (bi, 0, i, 0, 0)),
            pl.BlockSpec((1, tm, hw), lambda bi, i: (bi, i, 0)),
            pl.BlockSpec((1, ATT_KV_HEADS, hd + ONES_ROWS, tm), lambda bi, i: (bi, 0, 0, i)),
        ],
        out_shape=[
            jax.ShapeDtypeStruct((b, d // hw, n // tq, hw, tq), BF16),
            jax.ShapeDtypeStruct((b, n, hw), BF16),
            jax.ShapeDtypeStruct((b, ATT_KV_HEADS, hd + ONES_ROWS, n), BF16),
        ],
        scratch_shapes=[pltpu.VMEM((tm, d), BF16)],
        compiler_params=_params("arbitrary", "arbitrary"),
        name="att_in_proj",
    )(x, mod.table, gain, w, q_gain_a, q_gain_b, k_gain_a, k_gain_b, cos, sin)


def _attention_kernel(qt_ref, kx_ref, vt_ref, o_ref, s_scr, p_scr, *, group, tq):
    hd = ATT_HD
    nt = qt_ref.shape[2]
    hw = qt_ref.shape[3]
    row_group = (lax.broadcasted_iota(jnp.int32, (hw, 1), 0) // (hd // 2)) % ATT_KV_HEADS
    mine = row_group == pl.program_id(1)

    def rows(i):
        return pl.ds(pl.multiple_of(i * tq, tq), tq)

    def scores(i, g):
        slab = qt_ref[0, g, i]
        s_scr[g % 2] = _dot(kx_ref[0], jnp.where(mine, slab, jnp.zeros_like(slab)))

    def probs(g):
        m = jnp.max(s_scr[g % 2], axis=0, keepdims=True)
        p_scr[g % 2] = jnp.exp2(s_scr[g % 2] - m).astype(BF16)

    def weighted(i, g):
        acc = _dot(vt_ref[0, 0], p_scr[g % 2])
        out_t = acc[:hd] / acc[hd:hd + 1]
        o_ref[0, 0, rows(i), g * hd:(g + 1) * hd] = out_t.T.astype(BF16)

    scores(0, 0)

    def body(i, carry):
        for g in range(group):
            probs(g)
            if g + 1 < group:
                scores(i, g + 1)
            else:
                scores(jnp.minimum(i + 1, nt - 1), 0)
            weighted(i, g)
        return carry

    lax.fori_loop(0, nt, body, 0, unroll=ATT_TILE_UNROLL)


def _attention(qt, kx, vt):
    b, group, nt, hw, tq = qt.shape
    n = kx.shape[1]
    hd = ATT_HD
    assert group % 2 == 0
    return pl.pallas_call(
        functools.partial(_attention_kernel, group=group, tq=tq),
        grid=(b, ATT_KV_HEADS),
        in_specs=[
            pl.BlockSpec((1, group, nt, hw, tq), lambda bi, g: (bi, 0, 0, 0, 0)),
            pl.BlockSpec((1, n, hw), lambda bi, g: (bi, 0, 0)),
            pl.BlockSpec((1, 1, vt.shape[2], n), lambda bi, g: (bi, g, 0, 0)),
        ],
        out_specs=pl.BlockSpec((1, 1, n, group * hd), lambda bi, g: (bi, g, 0, 0)),
        out_shape=jax.ShapeDtypeStruct((b, ATT_KV_HEADS, n, group * hd), BF16),
        scratch_shapes=[pltpu.VMEM((2, n, tq), F32), pltpu.VMEM((2, n, tq), BF16)],
        compiler_params=_params("arbitrary", "arbitrary"),
        name="gqa_attention",
    )(qt, kx, vt)


def _rope_tables(n, head_dim, copies):
    rows = n // GRID_W
    nf = head_dim // 4
    inv = ROPE_THETA ** (-jnp.arange(nf, dtype=F32) / nf)
    ang_r = jnp.arange(rows, dtype=F32)[:, None] * inv[None, :]
    ang_c = jnp.arange(GRID_W, dtype=F32)[:, None] * inv[None, :]

    def table(fn):
        per_row = jnp.repeat(fn(ang_r), GRID_W, axis=0)
        per_col = jnp.tile(fn(ang_c), (rows, 1))
        return jnp.concatenate([per_row, per_col] * copies, axis=-1)

    return table(jnp.cos), table(jnp.sin)


def _trunk(x, mod_table, first_row, p):
    mod = ModRows(mod_table, 0, first_row)
    q, kt, v, gate = _ret_in(x, mod, p["norm1_g"][0], p["ret_w_in"], p["ret_cos"], p["ret_sin"])
    y = _retention(q, kt, v, p["ret_dec"])
    x = _out_mlp(x, y, gate, mod, p["ret_w_out"], p["norm2_g"][0], p["mlp_w1"][0], p["mlp_w2"][0],
                 p["final_g"], final=False)
    mod = ModRows(mod_table, 1, first_row)
    qt, kx, vt = _att_in(x, mod, p["norm1_g"][1], p["att_w_in"], p["att_q_gain_a"], p["att_q_gain_b"],
                         p["att_k_gain_a"], p["att_k_gain_b"], p["att_cos"], p["att_sin"])
    y = _attention(qt, kx, vt)
    return _out_mlp(x, y, None, mod, p["att_w_out"], p["norm2_g"][1], p["mlp_w1"][1], p["mlp_w2"][1],
                    p["final_g"], final=True)


def kernel(x_prompt, x_sample, c_prompt, c_sample, mod_w, mod_b, norm1_g, norm2_g, ret_w_in, ret_decay,
           ret_w_out, att_w_in, att_q_gain, att_k_gain, att_w_out, mlp_w1, mlp_w2, final_g):
    depth, d, _ = mod_w.shape
    assert depth == 2 and ret_w_in.shape[0] == 1 and att_w_in.shape[0] == 1
    bp, n, _ = x_prompt.shape
    assert x_sample.shape[1] == n

    c_all = jnp.concatenate([c_prompt, c_sample], axis=0)
    pad = (-c_all.shape[0]) % 16
    c_all = jnp.pad(c_all, ((0, pad), (0, 0)))
    mod = _adaln(c_all, mod_w, mod_b).reshape(depth, -1, N_MOD, d)

    dk = d // RET_HEADS
    dv = 2 * d // RET_HEADS
    w_ret = ret_w_in[0]
    ret_qk = w_ret[:, :2 * d].reshape(d, 2 * RET_HEADS, 2, 2, dk // 4).transpose(0, 1, 3, 2, 4)
    ret_w = jnp.concatenate([ret_qk.reshape(d, 2 * d), w_ret[:, 2 * d:]], axis=1)
    w_att = att_w_in[0]
    group = d // ATT_HD // ATT_KV_HEADS
    att_q = w_att[:, :d].reshape(d, ATT_KV_HEADS, group, 2, 2, ATT_HD // 4).transpose(0, 2, 4, 1, 3, 5)
    att_k = w_att[:, d:d + ATT_KV_HEADS * ATT_HD].reshape(d, ATT_KV_HEADS, 2, 2, ATT_HD // 4)
    att_k = att_k.transpose(0, 3, 1, 2, 4)
    att_w = jnp.concatenate([att_q.reshape(d, d), att_k.reshape(d, ATT_KV_HEADS * ATT_HD),
                             w_att[:, d + ATT_KV_HEADS * ATT_HD:]], axis=1)

    def slab_gain(gain, half):
        part = gain.reshape(2, 2, ATT_HD // 4)[:, half, :].reshape(-1)
        return jnp.tile(part, 2).reshape(1, ATT_HD)

    ret_cos, ret_sin = _rope_tables(n, dk, 1)
    att_cos, att_sin = _rope_tables(n, ATT_HD, 2)
    p = {
        "norm1_g": norm1_g.reshape(depth, 1, d),
        "norm2_g": norm2_g.reshape(depth, 1, d),
        "final_g": final_g.reshape(1, d),
        "ret_w_in": ret_w.astype(BF16),
        "ret_w_out": ret_w_out[0].astype(BF16),
        "att_w_in": att_w.astype(BF16),
        "att_w_out": att_w_out[0].astype(BF16),
        "att_q_gain_a": slab_gain(att_q_gain[0], 0),
        "att_q_gain_b": slab_gain(att_q_gain[0], 1),
        "att_k_gain_a": slab_gain(att_k_gain[0], 0),
        "att_k_gain_b": slab_gain(att_k_gain[0], 1),
        "mlp_w1": mlp_w1.astype(BF16),
        "mlp_w2": mlp_w2.astype(BF16),
        "ret_cos": ret_cos,
        "ret_sin": ret_sin,
        "att_cos": att_cos,
        "att_sin": att_sin,
        "ret_dec": jnp.pad(jnp.broadcast_to(ret_decay[0].T[:, :, None], (RET_HEADS, 2, dv)),
                           ((0, 0), (0, 6), (0, 0))),
    }
    y_prompt = _trunk(x_prompt, mod, 0, p)
    y_sample = _trunk(x_sample, mod, bp, p)
    return (y_prompt, y_sample)
```

```python
import functools
from typing import NamedTuple

import jax
import jax.numpy as jnp
from jax import lax
from jax.experimental import pallas as pl
from jax.experimental.pallas import tpu as pltpu

F32 = jnp.float32
BF16 = jnp.bfloat16

GRID_W = 64
ROPE_THETA = 10000.0
EPS = 1e-6
RET_HEADS = 4
ATT_HD = 128
ATT_KV_HEADS = 2
N_MOD = 6

V7X_VMEM_BYTES = 64 * 1024 * 1024
V7X_LANES = 128
V7X_MXU_DIM = 256

VMEM_LIMIT = V7X_VMEM_BYTES - 8 * 1024 * 1024
RET_TOK_TILE = 1024
SUB_TILE = 512
RET_SUB_TILE = 1024
ATT_TOK_TILE = 1024
MLP_TOK_TILE = 1024
FF_CHUNK = 1024
RET_CHUNK = V7X_MXU_DIM
SCAN_UNROLL = 5
OUT_UNROLL = 8
ATT_Q_TILE = 256
ATT_TILE_UNROLL = 4
ONES_ROWS = 16
LOG2_E = 1.4426950408889634


def _params(*sem):
    return pltpu.CompilerParams(dimension_semantics=sem, vmem_limit_bytes=VMEM_LIMIT)


def _resident(shape):
    nd = len(shape)
    return pl.BlockSpec(shape, lambda *_: (0,) * nd, pipeline_mode=pl.Buffered(1))


def _rms(x):
    return x * lax.rsqrt(jnp.mean(x * x, axis=-1, keepdims=True) + EPS)


def _dot(a, b):
    return jnp.dot(a, b, preferred_element_type=F32)


def _adaln_kernel(c_ref, w_ref, b_ref, o_ref):
    c = c_ref[...]
    s = c * jax.nn.sigmoid(c)
    w = w_ref[0]
    s_hi = s.astype(BF16)
    s_lo = (s - s_hi.astype(F32)).astype(BF16)
    w_hi = w.astype(BF16)
    w_lo = (w - w_hi.astype(F32)).astype(BF16)
    o_ref[0] = _dot(s_hi, w_hi) + _dot(s_lo, w_hi) + _dot(s_hi, w_lo) + b_ref[0]


def _adaln(c_all, mod_w, mod_b):
    depth, d, n6 = mod_w.shape
    rows = c_all.shape[0]
    tn = n6 // 4
    return pl.pallas_call(
        _adaln_kernel,
        grid=(depth, n6 // tn),
        in_specs=[
            pl.BlockSpec((rows, d), lambda i, j: (0, 0)),
            pl.BlockSpec((1, d, tn), lambda i, j: (i, 0, j)),
            pl.BlockSpec((1, 1, tn), lambda i, j: (i, 0, j)),
        ],
        out_specs=pl.BlockSpec((1, rows, tn), lambda i, j: (i, 0, j)),
        out_shape=jax.ShapeDtypeStruct((depth, rows, n6), F32),
        compiler_params=_params("arbitrary", "arbitrary"),
        name="adaln_mod",
    )(c_all, mod_w, mod_b.reshape(depth, 1, n6))


class ModRows(NamedTuple):
    table: jax.Array
    layer: int
    first_row: int


def _mod_spec(mod):
    _, _, n_mod, d = mod.table.shape
    return pl.BlockSpec((1, 1, n_mod, d), lambda bi, i: (mod.layer, mod.first_row + bi, 0, 0))


def _modulated_norm(x, gain, shift, scale):
    return (_rms(x) * gain * (1.0 + scale) + shift).astype(BF16)


def _ret_in_kernel(x_ref, mod_ref, g_ref, w_ref, cos_ref, sin_ref,
                   q_ref, kt_ref, v_ref, gate_ref, h_scr, *, d):
    m = mod_ref[0, 0]
    hw = V7X_MXU_DIM
    half = V7X_LANES
    c = RET_CHUNK
    ts = h_scr.shape[1]
    n_sub = h_scr.shape[0]
    v_off = 2 * d
    g_off = 4 * d

    def normalise(sub):
        rs = slice(sub * ts, (sub + 1) * ts)
        h_scr[sub] = _modulated_norm(x_ref[0, rs, :], g_ref[...], m[0:1], m[1:2])

    def project(sub):
        rs = slice(sub * ts, (sub + 1) * ts)
        cos = cos_ref[rs, :]
        sin = sin_ref[rs, :]
        for hh in list(range(RET_HEADS, 2 * RET_HEADS)) + list(range(RET_HEADS)):
            acc = _dot(h_scr[sub], w_ref[:, hh * hw:(hh + 1) * hw])
            if hh < RET_HEADS:
                acc = acc * (hw ** -0.5)
            a = acc[:, :half]
            b = acc[:, half:]
            ra = a * cos - b * sin
            rb = a * sin + b * cos
            if hh < RET_HEADS:
                q_ref[0, hh, rs, :half] = ra.astype(BF16)
                q_ref[0, hh, rs, half:] = rb.astype(BF16)
            else:
                for ci in range(ts // c):
                    kc = sub * (ts // c) + ci
                    kt_ref[0, hh - RET_HEADS, kc, :half, :] = ra[ci * c:(ci + 1) * c].T.astype(BF16)
                    kt_ref[0, hh - RET_HEADS, kc, half:, :] = rb[ci * c:(ci + 1) * c].T.astype(BF16)
        dv = v_ref.shape[-1]
        for hh in range(RET_HEADS):
            g = _dot(h_scr[sub], w_ref[:, g_off + hh * dv:g_off + (hh + 1) * dv])
            gate_ref[0, hh, rs, :] = (g * jax.nn.sigmoid(g)).astype(BF16)
        for hh in range(RET_HEADS):
            v = _dot(h_scr[sub], w_ref[:, v_off + hh * dv:v_off + (hh + 1) * dv])
            v_ref[0, hh, rs, :] = v.astype(BF16)

    for sub in range(n_sub):
        normalise(sub)
    for sub in range(n_sub):
        project(sub)


def _ret_in(x, mod, gain, w, cos, sin):
    b, n, d = x.shape
    tm = RET_TOK_TILE
    c = RET_CHUNK
    dk = d // RET_HEADS
    dv = 2 * d // RET_HEADS
    return pl.pallas_call(
        functools.partial(_ret_in_kernel, d=d),
        grid=(b, n // tm),
        in_specs=[
            pl.BlockSpec((1, tm, d), lambda bi, i: (bi, i, 0)),
            _mod_spec(mod),
            _resident(gain.shape),
            _resident(w.shape),
            pl.BlockSpec((tm, V7X_LANES), lambda bi, i: (i, 0)),
            pl.BlockSpec((tm, V7X_LANES), lambda bi, i: (i, 0)),
        ],
        out_specs=[
            pl.BlockSpec((1, RET_HEADS, tm, dk), lambda bi, i: (bi, 0, i, 0)),
            pl.BlockSpec((1, RET_HEADS, tm // c, dk, c), lambda bi, i: (bi, 0, i, 0, 0)),
            pl.BlockSpec((1, RET_HEADS, tm, dv), lambda bi, i: (bi, 0, i, 0)),
            pl.BlockSpec((1, RET_HEADS, tm, dv), lambda bi, i: (bi, 0, i, 0)),
        ],
        out_shape=[
            jax.ShapeDtypeStruct((b, RET_HEADS, n, dk), BF16),
            jax.ShapeDtypeStruct((b, RET_HEADS, n // c, dk, c), BF16),
            jax.ShapeDtypeStruct((b, RET_HEADS, n, dv), BF16),
            jax.ShapeDtypeStruct((b, RET_HEADS, n, dv), BF16),
        ],
        scratch_shapes=[pltpu.VMEM((tm // RET_SUB_TILE, RET_SUB_TILE, d), BF16)],
        compiler_params=_params("arbitrary", "arbitrary"),
        name="ret_in_proj",
    )(x, mod.table, gain, w, cos, sin)


def _retention_kernel(q_ref, kt_ref, v_ref, dec_ref, o_ref,
                      sf_all, sb_all, sf_scr, sb_scr, qdf_scr, qdb_scr, kdf_scr, kdb_scr, dm_scr, *, nc):
    c = RET_CHUNK
    dk = q_ref.shape[-1]
    dv = v_ref.shape[-1]
    dec = dec_ref[0]
    neg = -dec
    log_g = -(jnp.maximum(neg, 0.0) + jnp.log1p(jnp.exp(-jnp.abs(neg))))
    lf = log_g[0:1, :]
    lb = log_g[1:2, :]
    qpos = lax.broadcasted_iota(jnp.int32, (c, dk), 0).astype(F32)
    qdf_scr[...] = jnp.exp(lf[:, :dk] * (qpos + 1.0))
    qdb_scr[...] = jnp.exp(lb[:, :dk] * (c - qpos))
    kpos = lax.broadcasted_iota(jnp.int32, (dk, c), 1).astype(F32)
    kdf_scr[...] = jnp.exp(lf[:, :c] * (c - 1.0 - kpos))
    kdb_scr[...] = jnp.exp(lb[:, :c] * kpos)
    diff = (lax.broadcasted_iota(jnp.int32, (c, c), 0)
            - lax.broadcasted_iota(jnp.int32, (c, c), 1)).astype(F32)
    dm_scr[...] = jnp.where(diff >= 0.0,
                            jnp.exp(lf[:, :c] * jnp.maximum(diff, 0.0)),
                            jnp.exp(lb[:, :c] * jnp.maximum(-diff, 0.0)))
    cdf = jnp.exp(lf[:, :dv] * c)
    cdb = jnp.exp(lb[:, :dv] * c)

    def rows(i):
        return pl.ds(pl.multiple_of(i * c, c), c)

    sf_scr[...] = jnp.zeros_like(sf_scr)
    sb_scr[...] = jnp.zeros_like(sb_scr)
    sf_all[0] = jnp.zeros((dk, dv), BF16)
    sb_all[nc - 1] = jnp.zeros((dk, dv), BF16)

    def scan_body(t, carry):
        kf = (kt_ref[0, 0, t].astype(F32) * kdf_scr[...]).astype(BF16)
        sf = sf_scr[...] * cdf + _dot(kf, v_ref[0, 0, rows(t), :])
        sf_scr[...] = sf
        sf_all[t + 1] = sf.astype(BF16)
        j = nc - 1 - t
        kb = (kt_ref[0, 0, j].astype(F32) * kdb_scr[...]).astype(BF16)
        sb = sb_scr[...] * cdb + _dot(kb, v_ref[0, 0, rows(j), :])
        sb_scr[...] = sb
        sb_all[j - 1] = sb.astype(BF16)
        return carry

    lax.fori_loop(0, nc - 1, scan_body, 0, unroll=SCAN_UNROLL)

    def out_body(i, carry):
        r = rows(i)
        q = q_ref[0, 0, r, :]
        v = v_ref[0, 0, r, :]
        p = (_dot(q, kt_ref[0, 0, i]) * dm_scr[...]).astype(BF16)
        q32 = q.astype(F32)
        qf = (q32 * qdf_scr[...]).astype(BF16)
        qb = (q32 * qdb_scr[...]).astype(BF16)
        y = _dot(p, v) + _dot(qf, sf_all[i]) + _dot(qb, sb_all[i])
        mu = jnp.mean(y, axis=-1, keepdims=True)
        yc = y - mu
        var = jnp.mean(yc * yc, axis=-1, keepdims=True)
        yn = yc * lax.rsqrt(var + EPS)
        o_ref[0, 0, r, :] = yn.astype(BF16)
        return carry

    lax.fori_loop(0, nc, out_body, 0, unroll=OUT_UNROLL)


def _retention(q, kt, v, dec):
    b, _, n, dk = q.shape
    dv = v.shape[-1]
    c = RET_CHUNK
    nc = n // c
    return pl.pallas_call(
        functools.partial(_retention_kernel, nc=nc),
        grid=(b, RET_HEADS),
        in_specs=[
            pl.BlockSpec((1, 1, n, dk), lambda bi, h: (bi, h, 0, 0)),
            pl.BlockSpec((1, 1, nc, dk, c), lambda bi, h: (bi, h, 0, 0, 0)),
            pl.BlockSpec((1, 1, n, dv), lambda bi, h: (bi, h, 0, 0)),
            pl.BlockSpec((1, 8, dv), lambda bi, h: (h, 0, 0)),
        ],
        out_specs=pl.BlockSpec((1, 1, n, dv), lambda bi, h: (bi, h, 0, 0)),
        out_shape=jax.ShapeDtypeStruct((b, RET_HEADS, n, dv), BF16),
        scratch_shapes=[
            pltpu.VMEM((nc, dk, dv), BF16),
            pltpu.VMEM((nc, dk, dv), BF16),
            pltpu.VMEM((dk, dv), F32),
            pltpu.VMEM((dk, dv), F32),
            pltpu.VMEM((c, dk), F32),
            pltpu.VMEM((c, dk), F32),
            pltpu.VMEM((dk, c), F32),
            pltpu.VMEM((dk, c), F32),
            pltpu.VMEM((c, c), F32),
        ],
        compiler_params=_params("arbitrary", "arbitrary"),
        name="retention_core",
    )(q, kt, v, dec)


def _out_mlp_kernel(x_ref, y_ref, gate_ref, mod_ref, wo_ref, g2_ref, w1_ref, w2_ref, fg_ref, o_ref,
                    h_scr, *, final, gated):
    m = mod_ref[0, 0]
    ts = h_scr.shape[1]
    n_sub = h_scr.shape[0]
    d_ff = w1_ref.shape[1]

    def mix(sub):
        rs = slice(sub * ts, (sub + 1) * ts)
        dy = y_ref.shape[-1]
        mixed = None
        for hy in range(y_ref.shape[1]):
            yh = y_ref[0, hy, rs, :]
            if gated:
                yh = yh * gate_ref[0, hy, rs, :]
            part = _dot(yh, wo_ref[hy * dy:(hy + 1) * dy, :])
            mixed = part if mixed is None else mixed + part
        x1 = x_ref[0, rs, :] + m[2:3] * mixed
        o_ref[0, rs, :] = x1
        h_scr[sub] = _modulated_norm(x1, g2_ref[...], m[3:4], m[4:5])

    def mlp(sub):
        rs = slice(sub * ts, (sub + 1) * ts)
        for cc in range(d_ff // FF_CHUNK):
            cs = slice(cc * FF_CHUNK, (cc + 1) * FF_CHUNK)
            u = jnp.maximum(_dot(h_scr[sub], w1_ref[:, cs]), 0.0)
            o_ref[0, rs, :] += m[5:6] * _dot((u * u).astype(BF16), w2_ref[cs, :])
        if final:
            o_ref[0, rs, :] = _rms(o_ref[0, rs, :]) * fg_ref[...]

    for sub in range(n_sub):
        mix(sub)
    for sub in range(n_sub):
        mlp(sub)


def _out_mlp(x, y, gate, mod, w_out, gain2, w1, w2, final_g, final):
    b, n, d = x.shape
    tm = MLP_TOK_TILE
    _, hy, _, dy = y.shape
    y_spec = pl.BlockSpec((1, hy, tm, dy), lambda bi, i: (bi, 0, i, 0))
    gated = gate is not None
    if not gated:
        gate = jnp.zeros((1, 1, 8, V7X_LANES), BF16)
    gate_spec = y_spec if gated else _resident(gate.shape)
    return pl.pallas_call(
        functools.partial(_out_mlp_kernel, final=final, gated=gated),
        grid=(b, n // tm),
        in_specs=[
            pl.BlockSpec((1, tm, d), lambda bi, i: (bi, i, 0)),
            y_spec,
            gate_spec,
            _mod_spec(mod),
            _resident(w_out.shape),
            _resident(gain2.shape),
            _resident(w1.shape),
            _resident(w2.shape),
            _resident(final_g.shape),
        ],
        out_specs=pl.BlockSpec((1, tm, d), lambda bi, i: (bi, i, 0)),
        out_shape=jax.ShapeDtypeStruct((b, n, d), F32),
        scratch_shapes=[pltpu.VMEM((tm // SUB_TILE, SUB_TILE, d), BF16)],
        compiler_params=_params("arbitrary", "arbitrary"),
        name="out_proj_mlp_final" if final else "out_proj_mlp",
    )(x, y, gate, mod.table, w_out, gain2, w1, w2, final_g)


def _att_in_kernel(x_ref, mod_ref, g_ref, w_ref, qga_ref, qgb_ref, kga_ref, kgb_ref, cos_ref, sin_ref,
                   qt_ref, kx_ref, vt_ref, h_scr, *, d):
    m = mod_ref[0, 0]
    h_scr[...] = _modulated_norm(x_ref[0], g_ref[...], m[0:1], m[1:2])
    hd = ATT_HD
    hw = V7X_MXU_DIM
    half = hw // 2
    part = hd // 2
    n_q_slabs = d // hw
    cos = cos_ref[...]
    sin = sin_ref[...]

    def tables(ga, gb):
        return cos * ga, sin * gb, sin * ga, cos * gb

    q_scale = hd ** -0.5 * LOG2_E
    q_tab = tables(qga_ref[...] * q_scale, qgb_ref[...] * q_scale)
    k_tab = tables(kga_ref[...], kgb_ref[...])
    first = lax.broadcasted_iota(jnp.int32, (1, half), 1) < part
    tq = qt_ref.shape[-1]

    def values():
        v_off = (n_q_slabs + 1) * hw
        v = _dot(h_scr[...], w_ref[:, v_off:v_off + ATT_KV_HEADS * hd])
        for g in range(ATT_KV_HEADS):
            vt_ref[0, g, :hd, :] = v[:, g * hd:(g + 1) * hd].T.astype(BF16)
            vt_ref[0, g, hd:, :] = jnp.ones((vt_ref.shape[2] - hd, x_ref.shape[1]), BF16)

    for sl in range(n_q_slabs + 1):
        if sl == n_q_slabs:
            values()
        acc = _dot(h_scr[...], w_ref[:, sl * hw:(sl + 1) * hw])
        x = acc[:, :half]
        y = acc[:, half:]
        u = x * x + y * y
        ss0 = jnp.sum(jnp.where(first, u, 0.0), axis=-1, keepdims=True)
        ss1 = jnp.sum(jnp.where(first, 0.0, u), axis=-1, keepdims=True)
        r = lax.rsqrt(jnp.where(first, ss0, ss1) * (1.0 / hd) + EPS)
        c_a, s_b, s_a, c_b = q_tab if sl < n_q_slabs else k_tab
        ox = (x * c_a - y * s_b) * r
        oy = (x * s_a + y * c_b) * r
        if sl < n_q_slabs:
            for ti in range(x_ref.shape[1] // tq):
                qt_ref[0, sl, ti, :half, :] = ox[ti * tq:(ti + 1) * tq].T.astype(BF16)
                qt_ref[0, sl, ti, half:, :] = oy[ti * tq:(ti + 1) * tq].T.astype(BF16)
        else:
            kx_ref[0, :, :half] = ox.astype(BF16)
            kx_ref[0, :, half:] = oy.astype(BF16)


def _att_in(x, mod, gain, w, q_gain_a, q_gain_b, k_gain_a, k_gain_b, cos, sin):
    b, n, d = x.shape
    tm = ATT_TOK_TILE
    tq = ATT_Q_TILE
    hd = ATT_HD
    hw = V7X_MXU_DIM
    assert ATT_KV_HEADS * hd == hw
    return pl.pallas_call(
        functools.partial(_att_in_kernel, d=d),
        grid=(b, n // tm),
        in_specs=[
            pl.BlockSpec((1, tm, d), lambda bi, i: (bi, i, 0)),
            _mod_spec(mod),
            _resident(gain.shape),
            _resident(w.shape),
            _resident(q_gain_a.shape),
            _resident(q_gain_b.shape),
            _resident(k_gain_a.shape),
            _resident(k_gain_b.shape),
            pl.BlockSpec((tm, hd), lambda bi, i: (i, 0)),
            pl.BlockSpec((tm, hd), lambda bi, i: (i, 0)),
        ],
        out_specs=[
            pl.BlockSpec((1, d // hw, tm // tq, hw, tq), lambda bi, i: (bi, 0, i, 0, 0)),
            pl.BlockSpec((1, tm, hw), lambda bi, i: (bi, i, 0)),
            pl.BlockSpec((1, ATT_KV_HEADS, hd + ONES_ROWS, tm), lambda bi, i: (bi, 0, 0, i)),
        ],
        out_shape=[
            jax.ShapeDtypeStruct((b, d // hw, n // tq, hw, tq), BF16),
            jax.ShapeDtypeStruct((b, n, hw), BF16),
            jax.ShapeDtypeStruct((b, ATT_KV_HEADS, hd + ONES_ROWS, n), BF16),
        ],
        scratch_shapes=[pltpu.VMEM((tm, d), BF16)],
        compiler_params=_params("arbitrary", "arbitrary"),
        name="att_in_proj",
    )(x, mod.table, gain, w, q_gain_a, q_gain_b, k_gain_a, k_gain_b, cos, sin)


def _attention_kernel(qt_ref, kx_ref, vt_ref, o_ref, s_scr, p_scr, *, group, tq):
    hd = ATT_HD
    nt = qt_ref.shape[2]
    hw = qt_ref.shape[3]
    row_group = (lax.broadcasted_iota(jnp.int32, (hw, 1), 0) // (hd // 2)) % ATT_KV_HEADS
    mine = row_group == pl.program_id(1)

    def rows(i):
        return pl.ds(pl.multiple_of(i * tq, tq), tq)

    def scores(i, g):
        slab = qt_ref[0, g, i]
        s_scr[g % 2] = _dot(kx_ref[0], jnp.where(mine, slab, jnp.zeros_like(slab)))

    def probs(g):
        m = jnp.max(s_scr[g % 2], axis=0, keepdims=True)
        p_scr[g % 2] = jnp.exp2(s_scr[g % 2] - m).astype(BF16)

    def weighted(i, g):
        acc = _dot(vt_ref[0, 0], p_scr[g % 2])
        out_t = acc[:hd] / acc[hd:hd + 1]
        o_ref[0, 0, rows(i), g * hd:(g + 1) * hd] = out_t.T.astype(BF16)

    scores(0, 0)

    def body(i, carry):
        for g in range(group):
            probs(g)
            if g + 1 < group:
                scores(i, g + 1)
            else:
                scores(jnp.minimum(i + 1, nt - 1), 0)
            weighted(i, g)
        return carry

    lax.fori_loop(0, nt, body, 0, unroll=ATT_TILE_UNROLL)


def _attention(qt, kx, vt):
    b, group, nt, hw, tq = qt.shape
    n = kx.shape[1]
    hd = ATT_HD
    assert group % 2 == 0
    return pl.pallas_call(
        functools.partial(_attention_kernel, group=group, tq=tq),
        grid=(b, ATT_KV_HEADS),
        in_specs=[
            pl.BlockSpec((1, group, nt, hw, tq), lambda bi, g: (bi, 0, 0, 0, 0)),
            pl.BlockSpec((1, n, hw), lambda bi, g: (bi, 0, 0)),
            pl.BlockSpec((1, 1, vt.shape[2], n), lambda bi, g: (bi, g, 0, 0)),
        ],
        out_specs=pl.BlockSpec((1, 1, n, group * hd), lambda bi, g: (bi, g, 0, 0)),
        out_shape=jax.ShapeDtypeStruct((b, ATT_KV_HEADS, n, group * hd), BF16),
        scratch_shapes=[pltpu.VMEM((2, n, tq), F32), pltpu.VMEM((2, n, tq), BF16)],
        compiler_params=_params("arbitrary", "arbitrary"),
        name="gqa_attention",
    )(qt, kx, vt)


def _rope_tables(n, head_dim, copies):
    rows = n // GRID_W
    nf = head_dim // 4
    inv = ROPE_THETA ** (-jnp.arange(nf, dtype=F32) / nf)
    ang_r = jnp.arange(rows, dtype=F32)[:, None] * inv[None, :]
    ang_c = jnp.arange(GRID_W, dtype=F32)[:, None] * inv[None, :]

    def table(fn):
        per_row = jnp.repeat(fn(ang_r), GRID_W, axis=0)
        per_col = jnp.tile(fn(ang_c), (rows, 1))
        return jnp.concatenate([per_row, per_col] * copies, axis=-1)

    return table(jnp.cos), table(jnp.sin)


def _trunk(x, mod_table, first_row, p):
    mod = ModRows(mod_table, 0, first_row)
    q, kt, v, gate = _ret_in(x, mod, p["norm1_g"][0], p["ret_w_in"], p["ret_cos"], p["ret_sin"])
    y = _retention(q, kt, v, p["ret_dec"])
    x = _out_mlp(x, y, gate, mod, p["ret_w_out"], p["norm2_g"][0], p["mlp_w1"][0], p["mlp_w2"][0],
                 p["final_g"], final=False)
    mod = ModRows(mod_table, 1, first_row)
    qt, kx, vt = _att_in(x, mod, p["norm1_g"][1], p["att_w_in"], p["att_q_gain_a"], p["att_q_gain_b"],
                         p["att_k_gain_a"], p["att_k_gain_b"], p["att_cos"], p["att_sin"])
    y = _attention(qt, kx, vt)
    return _out_mlp(x, y, None, mod, p["att_w_out"], p["norm2_g"][1], p["mlp_w1"][1], p["mlp_w2"][1],
                    p["final_g"], final=True)


def kernel(x_prompt, x_sample, c_prompt, c_sample, mod_w, mod_b, norm1_g, norm2_g, ret_w_in, ret_decay,
           ret_w_out, att_w_in, att_q_gain, att_k_gain, att_w_out, mlp_w1, mlp_w2, final_g):
    depth, d, _ = mod_w.shape
    assert depth == 2 and ret_w_in.shape[0] == 1 and att_w_in.shape[0] == 1
    bp, n, _ = x_prompt.shape
    assert x_sample.shape[1] == n

    c_all = jnp.concatenate([c_prompt, c_sample], axis=0)
    pad = (-c_all.shape[0]) % 16
    c_all = jnp.pad(c_all, ((0, pad), (0, 0)))
    mod = _adaln(c_all, mod_w, mod_b).reshape(depth, -1, N_MOD, d)

    dk = d // RET_HEADS
    dv = 2 * d // RET_HEADS
    w_ret = ret_w_in[0]
    ret_qk = w_ret[:, :2 * d].reshape(d, 2 * RET_HEADS, 2, 2, dk // 4).transpose(0, 1, 3, 2, 4)
    ret_w = jnp.concatenate([ret_qk.reshape(d, 2 * d), w_ret[:, 2 * d:]], axis=1)
    w_att = att_w_in[0]
    group = d // ATT_HD // ATT_KV_HEADS
    att_q = w_att[:, :d].reshape(d, ATT_KV_HEADS, group, 2, 2, ATT_HD // 4).transpose(0, 2, 4, 1, 3, 5)
    att_k = w_att[:, d:d + ATT_KV_HEADS * ATT_HD].reshape(d, ATT_KV_HEADS, 2, 2, ATT_HD // 4)
    att_k = att_k.transpose(0, 3, 1, 2, 4)
    att_w = jnp.concatenate([att_q.reshape(d, d), att_k.reshape(d, ATT_KV_HEADS * ATT_HD),
                             w_att[:, d + ATT_KV_HEADS * ATT_HD:]], axis=1)

    def slab_gain(gain, half):
        part = gain.reshape(2, 2, ATT_HD // 4)[:, half, :].reshape(-1)
        return jnp.tile(part, 2).reshape(1, ATT_HD)

    ret_cos, ret_sin = _rope_tables(n, dk, 1)
    att_cos, att_sin = _rope_tables(n, ATT_HD, 2)
    p = {
        "norm1_g": norm1_g.reshape(depth, 1, d),
        "norm2_g": norm2_g.reshape(depth, 1, d),
        "final_g": final_g.reshape(1, d),
        "ret_w_in": ret_w.astype(BF16),
        "ret_w_out": ret_w_out[0].astype(BF16),
        "att_w_in": att_w.astype(BF16),
        "att_w_out": att_w_out[0].astype(BF16),
        "att_q_gain_a": slab_gain(att_q_gain[0], 0),
        "att_q_gain_b": slab_gain(att_q_gain[0], 1),
        "att_k_gain_a": slab_gain(att_k_gain[0], 0),
        "att_k_gain_b": slab_gain(att_k_gain[0], 1),
        "mlp_w1": mlp_w1.astype(BF16),
        "mlp_w2": mlp_w2.astype(BF16),
        "ret_cos": ret_cos,
        "ret_sin": ret_sin,
        "att_cos": att_cos,
        "att_sin": att_sin,
        "ret_dec": jnp.pad(jnp.broadcast_to(ret_decay[0].T[:, :, None], (RET_HEADS, 2, dv)),
                           ((0, 0), (0, 6), (0, 0))),
    }
    y_prompt = _trunk(x_prompt, mod, 0, p)
    y_sample = _trunk(x_sample, mod, bp, p)
    return (y_prompt, y_sample)
```

```python
import functools
from typing import NamedTuple

import jax
import jax.numpy as jnp
from jax import lax
from jax.experimental import pallas as pl
from jax.experimental.pallas import tpu as pltpu

F32 = jnp.float32
BF16 = jnp.bfloat16

GRID_W = 64
ROPE_THETA = 10000.0
EPS = 1e-6
RET_HEADS = 4
ATT_HD = 128
ATT_KV_HEADS = 2
N_MOD = 6

V7X_VMEM_BYTES = 64 * 1024 * 1024
V7X_LANES = 128
V7X_MXU_DIM = 256

VMEM_LIMIT = V7X_VMEM_BYTES - 8 * 1024 * 1024
RET_TOK_TILE = 1024
SUB_TILE = 512
RET_SUB_TILE = 1024
ATT_TOK_TILE = 1024
MLP_TOK_TILE = 1024
FF_CHUNK = 1024
RET_CHUNK = V7X_MXU_DIM
SCAN_UNROLL = 5
OUT_UNROLL = 8
ATT_Q_TILE = 256
ATT_TILE_UNROLL = 4
ATT_LOOKAHEAD = 2
ONES_ROWS = 16
LOG2_E = 1.4426950408889634


def _params(*sem):
    return pltpu.CompilerParams(dimension_semantics=sem, vmem_limit_bytes=VMEM_LIMIT)


def _resident(shape):
    nd = len(shape)
    return pl.BlockSpec(shape, lambda *_: (0,) * nd, pipeline_mode=pl.Buffered(1))


def _rms(x):
    return x * lax.rsqrt(jnp.mean(x * x, axis=-1, keepdims=True) + EPS)


def _dot(a, b):
    return jnp.dot(a, b, preferred_element_type=F32)


def _adaln_kernel(c_ref, w_ref, b_ref, o_ref):
    c = c_ref[...]
    s = c * jax.nn.sigmoid(c)
    w = w_ref[0]
    s_hi = s.astype(BF16)
    s_lo = (s - s_hi.astype(F32)).astype(BF16)
    w_hi = w.astype(BF16)
    w_lo = (w - w_hi.astype(F32)).astype(BF16)
    o_ref[0] = _dot(s_hi, w_hi) + _dot(s_lo, w_hi) + _dot(s_hi, w_lo) + b_ref[0]


def _adaln(c_all, mod_w, mod_b):
    depth, d, n6 = mod_w.shape
    rows = c_all.shape[0]
    tn = n6 // 4
    return pl.pallas_call(
        _adaln_kernel,
        grid=(depth, n6 // tn),
        in_specs=[
            pl.BlockSpec((rows, d), lambda i, j: (0, 0)),
            pl.BlockSpec((1, d, tn), lambda i, j: (i, 0, j)),
            pl.BlockSpec((1, 1, tn), lambda i, j: (i, 0, j)),
        ],
        out_specs=pl.BlockSpec((1, rows, tn), lambda i, j: (i, 0, j)),
        out_shape=jax.ShapeDtypeStruct((depth, rows, n6), F32),
        compiler_params=_params("arbitrary", "arbitrary"),
        name="adaln_mod",
    )(c_all, mod_w, mod_b.reshape(depth, 1, n6))


class ModRows(NamedTuple):
    table: jax.Array
    layer: int
    first_row: int


def _mod_spec(mod):
    _, _, n_mod, d = mod.table.shape
    return pl.BlockSpec((1, 1, n_mod, d), lambda bi, i: (mod.layer, mod.first_row + bi, 0, 0))


def _modulated_norm(x, gain, shift, scale):
    return (_rms(x) * gain * (1.0 + scale) + shift).astype(BF16)


def _ret_in_kernel(x_ref, mod_ref, g_ref, w_ref, cos_ref, sin_ref,
                   q_ref, kt_ref, v_ref, gate_ref, h_scr, *, d):
    m = mod_ref[0, 0]
    hw = V7X_MXU_DIM
    half = V7X_LANES
    c = RET_CHUNK
    ts = h_scr.shape[1]
    n_sub = h_scr.shape[0]
    v_off = 2 * d
    g_off = 4 * d

    def normalise(sub):
        rs = slice(sub * ts, (sub + 1) * ts)
        h_scr[sub] = _modulated_norm(x_ref[0, rs, :], g_ref[...], m[0:1], m[1:2])

    def project(sub):
        rs = slice(sub * ts, (sub + 1) * ts)
        cos = cos_ref[rs, :]
        sin = sin_ref[rs, :]
        for hh in list(range(RET_HEADS, 2 * RET_HEADS)) + list(range(RET_HEADS)):
            acc = _dot(h_scr[sub], w_ref[:, hh * hw:(hh + 1) * hw])
            if hh < RET_HEADS:
                acc = acc * (hw ** -0.5)
            a = acc[:, :half]
            b = acc[:, half:]
            ra = a * cos - b * sin
            rb = a * sin + b * cos
            if hh < RET_HEADS:
                q_ref[0, hh, rs, :half] = ra.astype(BF16)
                q_ref[0, hh, rs, half:] = rb.astype(BF16)
            else:
                for ci in range(ts // c):
                    kc = sub * (ts // c) + ci
                    kt_ref[0, hh - RET_HEADS, kc, :half, :] = ra[ci * c:(ci + 1) * c].T.astype(BF16)
                    kt_ref[0, hh - RET_HEADS, kc, half:, :] = rb[ci * c:(ci + 1) * c].T.astype(BF16)
        dv = v_ref.shape[-1]
        for hh in range(RET_HEADS):
            g = _dot(h_scr[sub], w_ref[:, g_off + hh * dv:g_off + (hh + 1) * dv])
            gate_ref[0, hh, rs, :] = (g * jax.nn.sigmoid(g)).astype(BF16)
        for hh in range(RET_HEADS):
            v = _dot(h_scr[sub], w_ref[:, v_off + hh * dv:v_off + (hh + 1) * dv])
            v_ref[0, hh, rs, :] = v.astype(BF16)

    for sub in range(n_sub):
        normalise(sub)
    for sub in range(n_sub):
        project(sub)


def _ret_in(x, mod, gain, w, cos, sin):
    b, n, d = x.shape
    tm = RET_TOK_TILE
    c = RET_CHUNK
    dk = d // RET_HEADS
    dv = 2 * d // RET_HEADS
    return pl.pallas_call(
        functools.partial(_ret_in_kernel, d=d),
        grid=(b, n // tm),
        in_specs=[
            pl.BlockSpec((1, tm, d), lambda bi, i: (bi, i, 0)),
            _mod_spec(mod),
            _resident(gain.shape),
            _resident(w.shape),
            pl.BlockSpec((tm, V7X_LANES), lambda bi, i: (i, 0)),
            pl.BlockSpec((tm, V7X_LANES), lambda bi, i: (i, 0)),
        ],
        out_specs=[
            pl.BlockSpec((1, RET_HEADS, tm, dk), lambda bi, i: (bi, 0, i, 0)),
            pl.BlockSpec((1, RET_HEADS, tm // c, dk, c), lambda bi, i: (bi, 0, i, 0, 0)),
            pl.BlockSpec((1, RET_HEADS, tm, dv), lambda bi, i: (bi, 0, i, 0)),
            pl.BlockSpec((1, RET_HEADS, tm, dv), lambda bi, i: (bi, 0, i, 0)),
        ],
        out_shape=[
            jax.ShapeDtypeStruct((b, RET_HEADS, n, dk), BF16),
            jax.ShapeDtypeStruct((b, RET_HEADS, n // c, dk, c), BF16),
            jax.ShapeDtypeStruct((b, RET_HEADS, n, dv), BF16),
            jax.ShapeDtypeStruct((b, RET_HEADS, n, dv), BF16),
        ],
        scratch_shapes=[pltpu.VMEM((tm // RET_SUB_TILE, RET_SUB_TILE, d), BF16)],
        compiler_params=_params("arbitrary", "arbitrary"),
        name="ret_in_proj",
    )(x, mod.table, gain, w, cos, sin)


def _retention_kernel(q_ref, kt_ref, v_ref, dec_ref, o_ref,
                      sf_all, sb_all, sf_scr, sb_scr, qdf_scr, qdb_scr, kdf_scr, kdb_scr, dm_scr, *, nc):
    c = RET_CHUNK
    dk = q_ref.shape[-1]
    dv = v_ref.shape[-1]
    dec = dec_ref[0]
    neg = -dec
    log_g = -(jnp.maximum(neg, 0.0) + jnp.log1p(jnp.exp(-jnp.abs(neg))))
    lf = log_g[0:1, :]
    lb = log_g[1:2, :]
    qpos = lax.broadcasted_iota(jnp.int32, (c, dk), 0).astype(F32)
    qdf_scr[...] = jnp.exp(lf[:, :dk] * (qpos + 1.0))
    qdb_scr[...] = jnp.exp(lb[:, :dk] * (c - qpos))
    kpos = lax.broadcasted_iota(jnp.int32, (dk, c), 1).astype(F32)
    kdf_scr[...] = jnp.exp(lf[:, :c] * (c - 1.0 - kpos))
    kdb_scr[...] = jnp.exp(lb[:, :c] * kpos)
    diff = (lax.broadcasted_iota(jnp.int32, (c, c), 0)
            - lax.broadcasted_iota(jnp.int32, (c, c), 1)).astype(F32)
    dm_scr[...] = jnp.where(diff >= 0.0,
                            jnp.exp(lf[:, :c] * jnp.maximum(diff, 0.0)),
                            jnp.exp(lb[:, :c] * jnp.maximum(-diff, 0.0)))
    cdf = jnp.exp(lf[:, :dv] * c)
    cdb = jnp.exp(lb[:, :dv] * c)

    def rows(i):
        return pl.ds(pl.multiple_of(i * c, c), c)

    sf_scr[...] = jnp.zeros_like(sf_scr)
    sb_scr[...] = jnp.zeros_like(sb_scr)
    sf_all[0] = jnp.zeros((dk, dv), BF16)
    sb_all[nc - 1] = jnp.zeros((dk, dv), BF16)

    def scan_body(t, carry):
        kf = (kt_ref[0, 0, t].astype(F32) * kdf_scr[...]).astype(BF16)
        sf = sf_scr[...] * cdf + _dot(kf, v_ref[0, 0, rows(t), :])
        sf_scr[...] = sf
        sf_all[t + 1] = sf.astype(BF16)
        j = nc - 1 - t
        kb = (kt_ref[0, 0, j].astype(F32) * kdb_scr[...]).astype(BF16)
        sb = sb_scr[...] * cdb + _dot(kb, v_ref[0, 0, rows(j), :])
        sb_scr[...] = sb
        sb_all[j - 1] = sb.astype(BF16)
        return carry

    lax.fori_loop(0, nc - 1, scan_body, 0, unroll=SCAN_UNROLL)

    def out_body(i, carry):
        r = rows(i)
        q = q_ref[0, 0, r, :]
        v = v_ref[0, 0, r, :]
        p = (_dot(q, kt_ref[0, 0, i]) * dm_scr[...]).astype(BF16)
        q32 = q.astype(F32)
        qf = (q32 * qdf_scr[...]).astype(BF16)
        qb = (q32 * qdb_scr[...]).astype(BF16)
        y = _dot(p, v) + _dot(qf, sf_all[i]) + _dot(qb, sb_all[i])
        mu = jnp.mean(y, axis=-1, keepdims=True)
        yc = y - mu
        var = jnp.mean(yc * yc, axis=-1, keepdims=True)
        yn = yc * lax.rsqrt(var + EPS)
        o_ref[0, 0, r, :] = yn.astype(BF16)
        return carry

    lax.fori_loop(0, nc, out_body, 0, unroll=OUT_UNROLL)


def _retention(q, kt, v, dec):
    b, _, n, dk = q.shape
    dv = v.shape[-1]
    c = RET_CHUNK
    nc = n // c
    return pl.pallas_call(
        functools.partial(_retention_kernel, nc=nc),
        grid=(b, RET_HEADS),
        in_specs=[
            pl.BlockSpec((1, 1, n, dk), lambda bi, h: (bi, h, 0, 0)),
            pl.BlockSpec((1, 1, nc, dk, c), lambda bi, h: (bi, h, 0, 0, 0)),
            pl.BlockSpec((1, 1, n, dv), lambda bi, h: (bi, h, 0, 0)),
            pl.BlockSpec((1, 8, dv), lambda bi, h: (h, 0, 0)),
        ],
        out_specs=pl.BlockSpec((1, 1, n, dv), lambda bi, h: (bi, h, 0, 0)),
        out_shape=jax.ShapeDtypeStruct((b, RET_HEADS, n, dv), BF16),
        scratch_shapes=[
            pltpu.VMEM((nc, dk, dv), BF16),
            pltpu.VMEM((nc, dk, dv), BF16),
            pltpu.VMEM((dk, dv), F32),
            pltpu.VMEM((dk, dv), F32),
            pltpu.VMEM((c, dk), F32),
            pltpu.VMEM((c, dk), F32),
            pltpu.VMEM((dk, c), F32),
            pltpu.VMEM((dk, c), F32),
            pltpu.VMEM((c, c), F32),
        ],
        compiler_params=_params("arbitrary", "arbitrary"),
        name="retention_core",
    )(q, kt, v, dec)


def _out_mlp_kernel(x_ref, y_ref, gate_ref, mod_ref, wo_ref, g2_ref, w1_ref, w2_ref, fg_ref, o_ref,
                    h_scr, *, final, gated):
    m = mod_ref[0, 0]
    ts = h_scr.shape[1]
    n_sub = h_scr.shape[0]
    d_ff = w1_ref.shape[1]

    def mix(sub):
        rs = slice(sub * ts, (sub + 1) * ts)
        dy = y_ref.shape[-1]
        mixed = None
        for hy in range(y_ref.shape[1]):
            yh = y_ref[0, hy, rs, :]
            if gated:
                yh = yh * gate_ref[0, hy, rs, :]
            part = _dot(yh, wo_ref[hy * dy:(hy + 1) * dy, :])
            mixed = part if mixed is None else mixed + part
        x1 = x_ref[0, rs, :] + m[2:3] * mixed
        o_ref[0, rs, :] = x1
        h_scr[sub] = _modulated_norm(x1, g2_ref[...], m[3:4], m[4:5])

    def mlp(sub):
        rs = slice(sub * ts, (sub + 1) * ts)
        for cc in range(d_ff // FF_CHUNK):
            cs = slice(cc * FF_CHUNK, (cc + 1) * FF_CHUNK)
            u = jnp.maximum(_dot(h_scr[sub], w1_ref[:, cs]), 0.0)
            o_ref[0, rs, :] += m[5:6] * _dot((u * u).astype(BF16), w2_ref[cs, :])
        if final:
            o_ref[0, rs, :] = _rms(o_ref[0, rs, :]) * fg_ref[...]

    for sub in range(n_sub):
        mix(sub)
    for sub in range(n_sub):
        mlp(sub)


def _out_mlp(x, y, gate, mod, w_out, gain2, w1, w2, final_g, final):
    b, n, d = x.shape
    tm = MLP_TOK_TILE
    _, hy, _, dy = y.shape
    y_spec = pl.BlockSpec((1, hy, tm, dy), lambda bi, i: (bi, 0, i, 0))
    gated = gate is not None
    if not gated:
        gate = jnp.zeros((1, 1, 8, V7X_LANES), BF16)
    gate_spec = y_spec if gated else _resident(gate.shape)
    return pl.pallas_call(
        functools.partial(_out_mlp_kernel, final=final, gated=gated),
        grid=(b, n // tm),
        in_specs=[
            pl.BlockSpec((1, tm, d), lambda bi, i: (bi, i, 0)),
            y_spec,
            gate_spec,
            _mod_spec(mod),
            _resident(w_out.shape),
            _resident(gain2.shape),
            _resident(w1.shape),
            _resident(w2.shape),
            _resident(final_g.shape),
        ],
        out_specs=pl.BlockSpec((1, tm, d), lambda bi, i: (bi, i, 0)),
        out_shape=jax.ShapeDtypeStruct((b, n, d), F32),
        scratch_shapes=[pltpu.VMEM((tm // SUB_TILE, SUB_TILE, d), BF16)],
        compiler_params=_params("arbitrary", "arbitrary"),
        name="out_proj_mlp_final" if final else "out_proj_mlp",
    )(x, y, gate, mod.table, w_out, gain2, w1, w2, final_g)


def _att_in_kernel(x_ref, mod_ref, g_ref, w_ref, qga_ref, qgb_ref, kga_ref, kgb_ref, cos_ref, sin_ref,
                   qt_ref, kx_ref, vt_ref, h_scr, *, d):
    m = mod_ref[0, 0]
    h_scr[...] = _modulated_norm(x_ref[0], g_ref[...], m[0:1], m[1:2])
    hd = ATT_HD
    hw = V7X_MXU_DIM
    half = hw // 2
    part = hd // 2
    n_q_slabs = d // hw
    cos = cos_ref[...]
    sin = sin_ref[...]

    def tables(ga, gb):
        return cos * ga, sin * gb, sin * ga, cos * gb

    q_scale = hd ** -0.5 * LOG2_E
    q_tab = tables(qga_ref[...] * q_scale, qgb_ref[...] * q_scale)
    k_tab = tables(kga_ref[...], kgb_ref[...])
    first = lax.broadcasted_iota(jnp.int32, (1, half), 1) < part
    tq = qt_ref.shape[-1]

    def values():
        v_off = (n_q_slabs + 1) * hw
        v = _dot(h_scr[...], w_ref[:, v_off:v_off + ATT_KV_HEADS * hd])
        for g in range(ATT_KV_HEADS):
            vt_ref[0, g, :hd, :] = v[:, g * hd:(g + 1) * hd].T.astype(BF16)
            vt_ref[0, g, hd:, :] = jnp.ones((vt_ref.shape[2] - hd, x_ref.shape[1]), BF16)

    for sl in range(n_q_slabs + 1):
        if sl == n_q_slabs:
            values()
        acc = _dot(h_scr[...], w_ref[:, sl * hw:(sl + 1) * hw])
        x = acc[:, :half]
        y = acc[:, half:]
        u = x * x + y * y
        ss0 = jnp.sum(jnp.where(first, u, 0.0), axis=-1, keepdims=True)
        ss1 = jnp.sum(jnp.where(first, 0.0, u), axis=-1, keepdims=True)
        r = lax.rsqrt(jnp.where(first, ss0, ss1) * (1.0 / hd) + EPS)
        c_a, s_b, s_a, c_b = q_tab if sl < n_q_slabs else k_tab
        ox = (x * c_a - y * s_b) * r
        oy = (x * s_a + y * c_b) * r
        if sl < n_q_slabs:
            for ti in range(x_ref.shape[1] // tq):
                qt_ref[0, sl, ti, :half, :] = ox[ti * tq:(ti + 1) * tq].T.astype(BF16)
                qt_ref[0, sl, ti, half:, :] = oy[ti * tq:(ti + 1) * tq].T.astype(BF16)
        else:
            kx_ref[0, :, :half] = ox.astype(BF16)
            kx_ref[0, :, half:] = oy.astype(BF16)


def _att_in(x, mod, gain, w, q_gain_a, q_gain_b, k_gain_a, k_gain_b, cos, sin):
    b, n, d = x.shape
    tm = ATT_TOK_TILE
    tq = ATT_Q_TILE
    hd = ATT_HD
    hw = V7X_MXU_DIM
    assert ATT_KV_HEADS * hd == hw
    return pl.pallas_call(
        functools.partial(_att_in_kernel, d=d),
        grid=(b, n // tm),
        in_specs=[
            pl.BlockSpec((1, tm, d), lambda bi, i: (bi, i, 0)),
            _mod_spec(mod),
            _resident(gain.shape),
            _resident(w.shape),
            _resident(q_gain_a.shape),
            _resident(q_gain_b.shape),
            _resident(k_gain_a.shape),
            _resident(k_gain_b.shape),
            pl.BlockSpec((tm, hd), lambda bi, i: (i, 0)),
            pl.BlockSpec((tm, hd), lambda bi, i: (i, 0)),
        ],
        out_specs=[
            pl.BlockSpec((1, d // hw, tm // tq, hw, tq), lambda bi, i: (bi, 0, i, 0, 0)),
            pl.BlockSpec((1, tm, hw), lambda bi, i: (bi, i, 0)),
            pl.BlockSpec((1, ATT_KV_HEADS, hd + ONES_ROWS, tm), lambda bi, i: (bi, 0, 0, i)),
        ],
        out_shape=[
            jax.ShapeDtypeStruct((b, d // hw, n // tq, hw, tq), BF16),
            jax.ShapeDtypeStruct((b, n, hw), BF16),
            jax.ShapeDtypeStruct((b, ATT_KV_HEADS, hd + ONES_ROWS, n), BF16),
        ],
        scratch_shapes=[pltpu.VMEM((tm, d), BF16)],
        compiler_params=_params("arbitrary", "arbitrary"),
        name="att_in_proj",
    )(x, mod.table, gain, w, q_gain_a, q_gain_b, k_gain_a, k_gain_b, cos, sin)


def _attention_kernel(qt_ref, kx_ref, vt_ref, o_ref, s_scr, p_scr, *, group, tq):
    hd = ATT_HD
    nt = qt_ref.shape[2]
    hw = qt_ref.shape[3]
    row_group = (lax.broadcasted_iota(jnp.int32, (hw, 1), 0) // (hd // 2)) % ATT_KV_HEADS
    mine = row_group == pl.program_id(1)

    def rows(i):
        return pl.ds(pl.multiple_of(i * tq, tq), tq)

    def scores(i, g):
        slab = qt_ref[0, g, i]
        s_scr[g] = _dot(kx_ref[0], jnp.where(mine, slab, jnp.zeros_like(slab)))

    def probs(g):
        m = jnp.max(s_scr[g], axis=0, keepdims=True)
        p_scr[g % 2] = jnp.exp2(s_scr[g] - m).astype(BF16)

    def weighted(i, g):
        acc = _dot(vt_ref[0, 0], p_scr[g % 2])
        out_t = acc[:hd] / acc[hd:hd + 1]
        o_ref[0, 0, rows(i), g * hd:(g + 1) * hd] = out_t.T.astype(BF16)

    for g in range(ATT_LOOKAHEAD):
        scores(0, g)

    def body(i, carry):
        for g in range(group):
            probs(g)
            ahead = g + ATT_LOOKAHEAD
            if ahead < group:
                scores(i, ahead)
            else:
                scores(jnp.minimum(i + 1, nt - 1), ahead - group)
            weighted(i, g)
        return carry

    lax.fori_loop(0, nt, body, 0, unroll=ATT_TILE_UNROLL)


def _attention(qt, kx, vt):
    b, group, nt, hw, tq = qt.shape
    n = kx.shape[1]
    hd = ATT_HD
    assert group % 2 == 0
    return pl.pallas_call(
        functools.partial(_attention_kernel, group=group, tq=tq),
        grid=(b, ATT_KV_HEADS),
        in_specs=[
            pl.BlockSpec((1, group, nt, hw, tq), lambda bi, g: (bi, 0, 0, 0, 0)),
            pl.BlockSpec((1, n, hw), lambda bi, g: (bi, 0, 0)),
            pl.BlockSpec((1, 1, vt.shape[2], n), lambda bi, g: (bi, g, 0, 0)),
        ],
        out_specs=pl.BlockSpec((1, 1, n, group * hd), lambda bi, g: (bi, g, 0, 0)),
        out_shape=jax.ShapeDtypeStruct((b, ATT_KV_HEADS, n, group * hd), BF16),
        scratch_shapes=[pltpu.VMEM((group, n, tq), F32), pltpu.VMEM((2, n, tq), BF16)],
        compiler_params=_params("arbitrary", "arbitrary"),
        name="gqa_attention",
    )(qt, kx, vt)


def _rope_tables(n, head_dim, copies):
    rows = n // GRID_W
    nf = head_dim // 4
    inv = ROPE_THETA ** (-jnp.arange(nf, dtype=F32) / nf)
    ang_r = jnp.arange(rows, dtype=F32)[:, None] * inv[None, :]
    ang_c = jnp.arange(GRID_W, dtype=F32)[:, None] * inv[None, :]

    def table(fn):
        per_row = jnp.repeat(fn(ang_r), GRID_W, axis=0)
        per_col = jnp.tile(fn(ang_c), (rows, 1))
        return jnp.concatenate([per_row, per_col] * copies, axis=-1)

    return table(jnp.cos), table(jnp.sin)


def _trunk(x, mod_table, first_row, p):
    mod = ModRows(mod_table, 0, first_row)
    q, kt, v, gate = _ret_in(x, mod, p["norm1_g"][0], p["ret_w_in"], p["ret_cos"], p["ret_sin"])
    y = _retention(q, kt, v, p["ret_dec"])
    x = _out_mlp(x, y, gate, mod, p["ret_w_out"], p["norm2_g"][0], p["mlp_w1"][0], p["mlp_w2"][0],
                 p["final_g"], final=False)
    mod = ModRows(mod_table, 1, first_row)
    qt, kx, vt = _att_in(x, mod, p["norm1_g"][1], p["att_w_in"], p["att_q_gain_a"], p["att_q_gain_b"],
                         p["att_k_gain_a"], p["att_k_gain_b"], p["att_cos"], p["att_sin"])
    y = _attention(qt, kx, vt)
    return _out_mlp(x, y, None, mod, p["att_w_out"], p["norm2_g"][1], p["mlp_w1"][1], p["mlp_w2"][1],
                    p["final_g"], final=True)


def kernel(x_prompt, x_sample, c_prompt, c_sample, mod_w, mod_b, norm1_g, norm2_g, ret_w_in, ret_decay,
           ret_w_out, att_w_in, att_q_gain, att_k_gain, att_w_out, mlp_w1, mlp_w2, final_g):
    depth, d, _ = mod_w.shape
    assert depth == 2 and ret_w_in.shape[0] == 1 and att_w_in.shape[0] == 1
    bp, n, _ = x_prompt.shape
    assert x_sample.shape[1] == n

    c_all = jnp.concatenate([c_prompt, c_sample], axis=0)
    pad = (-c_all.shape[0]) % 16
    c_all = jnp.pad(c_all, ((0, pad), (0, 0)))
    mod = _adaln(c_all, mod_w, mod_b).reshape(depth, -1, N_MOD, d)

    dk = d // RET_HEADS
    dv = 2 * d // RET_HEADS
    w_ret = ret_w_in[0]
    ret_qk = w_ret[:, :2 * d].reshape(d, 2 * RET_HEADS, 2, 2, dk // 4).transpose(0, 1, 3, 2, 4)
    ret_w = jnp.concatenate([ret_qk.reshape(d, 2 * d), w_ret[:, 2 * d:]], axis=1)
    w_att = att_w_in[0]
    group = d // ATT_HD // ATT_KV_HEADS
    att_q = w_att[:, :d].reshape(d, ATT_KV_HEADS, group, 2, 2, ATT_HD // 4).transpose(0, 2, 4, 1, 3, 5)
    att_k = w_att[:, d:d + ATT_KV_HEADS * ATT_HD].reshape(d, ATT_KV_HEADS, 2, 2, ATT_HD // 4)
    att_k = att_k.transpose(0, 3, 1, 2, 4)
    att_w = jnp.concatenate([att_q.reshape(d, d), att_k.reshape(d, ATT_KV_HEADS * ATT_HD),
                             w_att[:, d + ATT_KV_HEADS * ATT_HD:]], axis=1)

    def slab_gain(gain, half):
        part = gain.reshape(2, 2, ATT_HD // 4)[:, half, :].reshape(-1)
        return jnp.tile(part, 2).reshape(1, ATT_HD)

    ret_cos, ret_sin = _rope_tables(n, dk, 1)
    att_cos, att_sin = _rope_tables(n, ATT_HD, 2)
    p = {
        "norm1_g": norm1_g.reshape(depth, 1, d),
        "norm2_g": norm2_g.reshape(depth, 1, d),
        "final_g": final_g.reshape(1, d),
        "ret_w_in": ret_w.astype(BF16),
        "ret_w_out": ret_w_out[0].astype(BF16),
        "att_w_in": att_w.astype(BF16),
        "att_w_out": att_w_out[0].astype(BF16),
        "att_q_gain_a": slab_gain(att_q_gain[0], 0),
        "att_q_gain_b": slab_gain(att_q_gain[0], 1),
        "att_k_gain_a": slab_gain(att_k_gain[0], 0),
        "att_k_gain_b": slab_gain(att_k_gain[0], 1),
        "mlp_w1": mlp_w1.astype(BF16),
        "mlp_w2": mlp_w2.astype(BF16),
        "ret_cos": ret_cos,
        "ret_sin": ret_sin,
        "att_cos": att_cos,
        "att_sin": att_sin,
        "ret_dec": jnp.pad(jnp.broadcast_to(ret_decay[0].T[:, :, None], (RET_HEADS, 2, dv)),
                           ((0, 0), (0, 6), (0, 0))),
    }
    y_prompt = _trunk(x_prompt, mod, 0, p)
    y_sample = _trunk(x_sample, mod, bp, p)
    return (y_prompt, y_sample)
```

```python
import functools
from typing import NamedTuple

import jax
import jax.numpy as jnp
from jax import lax
from jax.experimental import pallas as pl
from jax.experimental.pallas import tpu as pltpu

F32 = jnp.float32
BF16 = jnp.bfloat16

GRID_W = 64
ROPE_THETA = 10000.0
EPS = 1e-6
RET_HEADS = 4
ATT_HD = 128
ATT_KV_HEADS = 2
N_MOD = 6

V7X_VMEM_BYTES = 64 * 1024 * 1024
V7X_LANES = 128
V7X_MXU_DIM = 256

VMEM_LIMIT = V7X_VMEM_BYTES - 8 * 1024 * 1024
RET_TOK_TILE = 1024
SUB_TILE = 512
RET_SUB_TILE = 1024
ATT_TOK_TILE = 1024
MLP_TOK_TILE = 1024
FF_CHUNK = 1024
RET_CHUNK = V7X_MXU_DIM
SCAN_UNROLL = 5
OUT_UNROLL = 8
ATT_Q_TILE = 256
ATT_TILE_UNROLL = 4
ATT_LOOKAHEAD = 3
ONES_ROWS = 16
LOG2_E = 1.4426950408889634


def _params(*sem):
    return pltpu.CompilerParams(dimension_semantics=sem, vmem_limit_bytes=VMEM_LIMIT)


def _resident(shape):
    nd = len(shape)
    return pl.BlockSpec(shape, lambda *_: (0,) * nd, pipeline_mode=pl.Buffered(1))


def _rms(x):
    return x * lax.rsqrt(jnp.mean(x * x, axis=-1, keepdims=True) + EPS)


def _dot(a, b):
    return jnp.dot(a, b, preferred_element_type=F32)


def _adaln_kernel(c_ref, w_ref, b_ref, o_ref):
    c = c_ref[...]
    s = c * jax.nn.sigmoid(c)
    w = w_ref[0]
    s_hi = s.astype(BF16)
    s_lo = (s - s_hi.astype(F32)).astype(BF16)
    w_hi = w.astype(BF16)
    w_lo = (w - w_hi.astype(F32)).astype(BF16)
    o_ref[0] = _dot(s_hi, w_hi) + _dot(s_lo, w_hi) + _dot(s_hi, w_lo) + b_ref[0]


def _adaln(c_all, mod_w, mod_b):
    depth, d, n6 = mod_w.shape
    rows = c_all.shape[0]
    tn = n6 // 4
    return pl.pallas_call(
        _adaln_kernel,
        grid=(depth, n6 // tn),
        in_specs=[
            pl.BlockSpec((rows, d), lambda i, j: (0, 0)),
            pl.BlockSpec((1, d, tn), lambda i, j: (i, 0, j)),
            pl.BlockSpec((1, 1, tn), lambda i, j: (i, 0, j)),
        ],
        out_specs=pl.BlockSpec((1, rows, tn), lambda i, j: (i, 0, j)),
        out_shape=jax.ShapeDtypeStruct((depth, rows, n6), F32),
        compiler_params=_params("arbitrary", "arbitrary"),
        name="adaln_mod",
    )(c_all, mod_w, mod_b.reshape(depth, 1, n6))


class ModRows(NamedTuple):
    table: jax.Array
    layer: int
    first_row: int


def _mod_spec(mod):
    _, _, n_mod, d = mod.table.shape
    return pl.BlockSpec((1, 1, n_mod, d), lambda bi, i: (mod.layer, mod.first_row + bi, 0, 0))


def _modulated_norm(x, gain, shift, scale):
    return (_rms(x) * gain * (1.0 + scale) + shift).astype(BF16)


def _ret_in_kernel(x_ref, mod_ref, g_ref, w_ref, cos_ref, sin_ref,
                   q_ref, kt_ref, v_ref, gate_ref, h_scr, *, d):
    m = mod_ref[0, 0]
    hw = V7X_MXU_DIM
    half = V7X_LANES
    c = RET_CHUNK
    ts = h_scr.shape[1]
    n_sub = h_scr.shape[0]
    v_off = 2 * d
    g_off = 4 * d

    def normalise(sub):
        rs = slice(sub * ts, (sub + 1) * ts)
        h_scr[sub] = _modulated_norm(x_ref[0, rs, :], g_ref[...], m[0:1], m[1:2])

    def project(sub):
        rs = slice(sub * ts, (sub + 1) * ts)
        cos = cos_ref[rs, :]
        sin = sin_ref[rs, :]
        for hh in list(range(RET_HEADS, 2 * RET_HEADS)) + list(range(RET_HEADS)):
            acc = _dot(h_scr[sub], w_ref[:, hh * hw:(hh + 1) * hw])
            if hh < RET_HEADS:
                acc = acc * (hw ** -0.5)
            a = acc[:, :half]
            b = acc[:, half:]
            ra = a * cos - b * sin
            rb = a * sin + b * cos
            if hh < RET_HEADS:
                q_ref[0, hh, rs, :half] = ra.astype(BF16)
                q_ref[0, hh, rs, half:] = rb.astype(BF16)
            else:
                for ci in range(ts // c):
                    kc = sub * (ts // c) + ci
                    kt_ref[0, hh - RET_HEADS, kc, :half, :] = ra[ci * c:(ci + 1) * c].T.astype(BF16)
                    kt_ref[0, hh - RET_HEADS, kc, half:, :] = rb[ci * c:(ci + 1) * c].T.astype(BF16)
        dv = v_ref.shape[-1]
        for hh in range(RET_HEADS):
            g = _dot(h_scr[sub], w_ref[:, g_off + hh * dv:g_off + (hh + 1) * dv])
            gate_ref[0, hh, rs, :] = (g * jax.nn.sigmoid(g)).astype(BF16)
        for hh in range(RET_HEADS):
            v = _dot(h_scr[sub], w_ref[:, v_off + hh * dv:v_off + (hh + 1) * dv])
            v_ref[0, hh, rs, :] = v.astype(BF16)

    for sub in range(n_sub):
        normalise(sub)
    for sub in range(n_sub):
        project(sub)


def _ret_in(x, mod, gain, w, cos, sin):
    b, n, d = x.shape
    tm = RET_TOK_TILE
    c = RET_CHUNK
    dk = d // RET_HEADS
    dv = 2 * d // RET_HEADS
    return pl.pallas_call(
        functools.partial(_ret_in_kernel, d=d),
        grid=(b, n // tm),
        in_specs=[
            pl.BlockSpec((1, tm, d), lambda bi, i: (bi, i, 0)),
            _mod_spec(mod),
            _resident(gain.shape),
            _resident(w.shape),
            pl.BlockSpec((tm, V7X_LANES), lambda bi, i: (i, 0)),
            pl.BlockSpec((tm, V7X_LANES), lambda bi, i: (i, 0)),
        ],
        out_specs=[
            pl.BlockSpec((1, RET_HEADS, tm, dk), lambda bi, i: (bi, 0, i, 0)),
            pl.BlockSpec((1, RET_HEADS, tm // c, dk, c), lambda bi, i: (bi, 0, i, 0, 0)),
            pl.BlockSpec((1, RET_HEADS, tm, dv), lambda bi, i: (bi, 0, i, 0)),
            pl.BlockSpec((1, RET_HEADS, tm, dv), lambda bi, i: (bi, 0, i, 0)),
        ],
        out_shape=[
            jax.ShapeDtypeStruct((b, RET_HEADS, n, dk), BF16),
            jax.ShapeDtypeStruct((b, RET_HEADS, n // c, dk, c), BF16),
            jax.ShapeDtypeStruct((b, RET_HEADS, n, dv), BF16),
            jax.ShapeDtypeStruct((b, RET_HEADS, n, dv), BF16),
        ],
        scratch_shapes=[pltpu.VMEM((tm // RET_SUB_TILE, RET_SUB_TILE, d), BF16)],
        compiler_params=_params("arbitrary", "arbitrary"),
        name="ret_in_proj",
    )(x, mod.table, gain, w, cos, sin)


def _retention_kernel(q_ref, kt_ref, v_ref, dec_ref, o_ref,
                      sf_all, sb_all, sf_scr, sb_scr, qdf_scr, qdb_scr, kdf_scr, kdb_scr, dm_scr, *, nc):
    c = RET_CHUNK
    dk = q_ref.shape[-1]
    dv = v_ref.shape[-1]
    dec = dec_ref[0]
    neg = -dec
    log_g = -(jnp.maximum(neg, 0.0) + jnp.log1p(jnp.exp(-jnp.abs(neg))))
    lf = log_g[0:1, :]
    lb = log_g[1:2, :]
    qpos = lax.broadcasted_iota(jnp.int32, (c, dk), 0).astype(F32)
    qdf_scr[...] = jnp.exp(lf[:, :dk] * (qpos + 1.0))
    qdb_scr[...] = jnp.exp(lb[:, :dk] * (c - qpos))
    kpos = lax.broadcasted_iota(jnp.int32, (dk, c), 1).astype(F32)
    kdf_scr[...] = jnp.exp(lf[:, :c] * (c - 1.0 - kpos))
    kdb_scr[...] = jnp.exp(lb[:, :c] * kpos)
    diff = (lax.broadcasted_iota(jnp.int32, (c, c), 0)
            - lax.broadcasted_iota(jnp.int32, (c, c), 1)).astype(F32)
    dm_scr[...] = jnp.where(diff >= 0.0,
                            jnp.exp(lf[:, :c] * jnp.maximum(diff, 0.0)),
                            jnp.exp(lb[:, :c] * jnp.maximum(-diff, 0.0)))
    cdf = jnp.exp(lf[:, :dv] * c)
    cdb = jnp.exp(lb[:, :dv] * c)

    def rows(i):
        return pl.ds(pl.multiple_of(i * c, c), c)

    sf_scr[...] = jnp.zeros_like(sf_scr)
    sb_scr[...] = jnp.zeros_like(sb_scr)
    sf_all[0] = jnp.zeros((dk, dv), BF16)
    sb_all[nc - 1] = jnp.zeros((dk, dv), BF16)

    def scan_body(t, carry):
        kf = (kt_ref[0, 0, t].astype(F32) * kdf_scr[...]).astype(BF16)
        sf = sf_scr[...] * cdf + _dot(kf, v_ref[0, 0, rows(t), :])
        sf_scr[...] = sf
        sf_all[t + 1] = sf.astype(BF16)
        j = nc - 1 - t
        kb = (kt_ref[0, 0, j].astype(F32) * kdb_scr[...]).astype(BF16)
        sb = sb_scr[...] * cdb + _dot(kb, v_ref[0, 0, rows(j), :])
        sb_scr[...] = sb
        sb_all[j - 1] = sb.astype(BF16)
        return carry

    lax.fori_loop(0, nc - 1, scan_body, 0, unroll=SCAN_UNROLL)

    def out_body(i, carry):
        r = rows(i)
        q = q_ref[0, 0, r, :]
        v = v_ref[0, 0, r, :]
        p = (_dot(q, kt_ref[0, 0, i]) * dm_scr[...]).astype(BF16)
        q32 = q.astype(F32)
        qf = (q32 * qdf_scr[...]).astype(BF16)
        qb = (q32 * qdb_scr[...]).astype(BF16)
        y = _dot(p, v) + _dot(qf, sf_all[i]) + _dot(qb, sb_all[i])
        mu = jnp.mean(y, axis=-1, keepdims=True)
        yc = y - mu
        var = jnp.mean(yc * yc, axis=-1, keepdims=True)
        yn = yc * lax.rsqrt(var + EPS)
        o_ref[0, 0, r, :] = yn.astype(BF16)
        return carry

    lax.fori_loop(0, nc, out_body, 0, unroll=OUT_UNROLL)


def _retention(q, kt, v, dec):
    b, _, n, dk = q.shape
    dv = v.shape[-1]
    c = RET_CHUNK
    nc = n // c
    return pl.pallas_call(
        functools.partial(_retention_kernel, nc=nc),
        grid=(b, RET_HEADS),
        in_specs=[
            pl.BlockSpec((1, 1, n, dk), lambda bi, h: (bi, h, 0, 0)),
            pl.BlockSpec((1, 1, nc, dk, c), lambda bi, h: (bi, h, 0, 0, 0)),
            pl.BlockSpec((1, 1, n, dv), lambda bi, h: (bi, h, 0, 0)),
            pl.BlockSpec((1, 8, dv), lambda bi, h: (h, 0, 0)),
        ],
        out_specs=pl.BlockSpec((1, 1, n, dv), lambda bi, h: (bi, h, 0, 0)),
        out_shape=jax.ShapeDtypeStruct((b, RET_HEADS, n, dv), BF16),
        scratch_shapes=[
            pltpu.VMEM((nc, dk, dv), BF16),
            pltpu.VMEM((nc, dk, dv), BF16),
            pltpu.VMEM((dk, dv), F32),
            pltpu.VMEM((dk, dv), F32),
            pltpu.VMEM((c, dk), F32),
            pltpu.VMEM((c, dk), F32),
            pltpu.VMEM((dk, c), F32),
            pltpu.VMEM((dk, c), F32),
            pltpu.VMEM((c, c), F32),
        ],
        compiler_params=_params("arbitrary", "arbitrary"),
        name="retention_core",
    )(q, kt, v, dec)


def _out_mlp_kernel(x_ref, y_ref, gate_ref, mod_ref, wo_ref, g2_ref, w1_ref, w2_ref, fg_ref, o_ref,
                    h_scr, *, final, gated):
    m = mod_ref[0, 0]
    ts = h_scr.shape[1]
    n_sub = h_scr.shape[0]
    d_ff = w1_ref.shape[1]

    def mix(sub):
        rs = slice(sub * ts, (sub + 1) * ts)
        dy = y_ref.shape[-1]
        mixed = None
        for hy in range(y_ref.shape[1]):
            yh = y_ref[0, hy, rs, :]
            if gated:
                yh = yh * gate_ref[0, hy, rs, :]
            part = _dot(yh, wo_ref[hy * dy:(hy + 1) * dy, :])
            mixed = part if mixed is None else mixed + part
        x1 = x_ref[0, rs, :] + m[2:3] * mixed
        o_ref[0, rs, :] = x1
        h_scr[sub] = _modulated_norm(x1, g2_ref[...], m[3:4], m[4:5])

    def mlp(sub):
        rs = slice(sub * ts, (sub + 1) * ts)
        for cc in range(d_ff // FF_CHUNK):
            cs = slice(cc * FF_CHUNK, (cc + 1) * FF_CHUNK)
            u = jnp.maximum(_dot(h_scr[sub], w1_ref[:, cs]), 0.0)
            o_ref[0, rs, :] += m[5:6] * _dot((u * u).astype(BF16), w2_ref[cs, :])
        if final:
            o_ref[0, rs, :] = _rms(o_ref[0, rs, :]) * fg_ref[...]

    for sub in range(n_sub):
        mix(sub)
    for sub in range(n_sub):
        mlp(sub)


def _out_mlp(x, y, gate, mod, w_out, gain2, w1, w2, final_g, final):
    b, n, d = x.shape
    tm = MLP_TOK_TILE
    _, hy, _, dy = y.shape
    y_spec = pl.BlockSpec((1, hy, tm, dy), lambda bi, i: (bi, 0, i, 0))
    gated = gate is not None
    if not gated:
        gate = jnp.zeros((1, 1, 8, V7X_LANES), BF16)
    gate_spec = y_spec if gated else _resident(gate.shape)
    return pl.pallas_call(
        functools.partial(_out_mlp_kernel, final=final, gated=gated),
        grid=(b, n // tm),
        in_specs=[
            pl.BlockSpec((1, tm, d), lambda bi, i: (bi, i, 0)),
            y_spec,
            gate_spec,
            _mod_spec(mod),
            _resident(w_out.shape),
            _resident(gain2.shape),
            _resident(w1.shape),
            _resident(w2.shape),
            _resident(final_g.shape),
        ],
        out_specs=pl.BlockSpec((1, tm, d), lambda bi, i: (bi, i, 0)),
        out_shape=jax.ShapeDtypeStruct((b, n, d), F32),
        scratch_shapes=[pltpu.VMEM((tm // SUB_TILE, SUB_TILE, d), BF16)],
        compiler_params=_params("arbitrary", "arbitrary"),
        name="out_proj_mlp_final" if final else "out_proj_mlp",
    )(x, y, gate, mod.table, w_out, gain2, w1, w2, final_g)


def _att_in_kernel(x_ref, mod_ref, g_ref, w_ref, qga_ref, qgb_ref, kga_ref, kgb_ref, cos_ref, sin_ref,
                   qt_ref, kx_ref, vt_ref, h_scr, *, d):
    m = mod_ref[0, 0]
    h_scr[...] = _modulated_norm(x_ref[0], g_ref[...], m[0:1], m[1:2])
    hd = ATT_HD
    hw = V7X_MXU_DIM
    half = hw // 2
    part = hd // 2
    n_q_slabs = d // hw
    cos = cos_ref[...]
    sin = sin_ref[...]

    def tables(ga, gb):
        return cos * ga, sin * gb, sin * ga, cos * gb

    q_scale = hd ** -0.5 * LOG2_E
    q_tab = tables(qga_ref[...] * q_scale, qgb_ref[...] * q_scale)
    k_tab = tables(kga_ref[...], kgb_ref[...])
    first = lax.broadcasted_iota(jnp.int32, (1, half), 1) < part
    tq = qt_ref.shape[-1]

    def values():
        v_off = (n_q_slabs + 1) * hw
        v = _dot(h_scr[...], w_ref[:, v_off:v_off + ATT_KV_HEADS * hd])
        for g in range(ATT_KV_HEADS):
            vt_ref[0, g, :hd, :] = v[:, g * hd:(g + 1) * hd].T.astype(BF16)
            vt_ref[0, g, hd:, :] = jnp.ones((vt_ref.shape[2] - hd, x_ref.shape[1]), BF16)

    for sl in range(n_q_slabs + 1):
        if sl == n_q_slabs:
            values()
        acc = _dot(h_scr[...], w_ref[:, sl * hw:(sl + 1) * hw])
        x = acc[:, :half]
        y = acc[:, half:]
        u = x * x + y * y
        ss0 = jnp.sum(jnp.where(first, u, 0.0), axis=-1, keepdims=True)
        ss1 = jnp.sum(jnp.where(first, 0.0, u), axis=-1, keepdims=True)
        r = lax.rsqrt(jnp.where(first, ss0, ss1) * (1.0 / hd) + EPS)
        c_a, s_b, s_a, c_b = q_tab if sl < n_q_slabs else k_tab
        ox = (x * c_a - y * s_b) * r
        oy = (x * s_a + y * c_b) * r
        if sl < n_q_slabs:
            for ti in range(x_ref.shape[1] // tq):
                qt_ref[0, sl, ti, :half, :] = ox[ti * tq:(ti + 1) * tq].T.astype(BF16)
                qt_ref[0, sl, ti, half:, :] = oy[ti * tq:(ti + 1) * tq].T.astype(BF16)
        else:
            kx_ref[0, :, :half] = ox.astype(BF16)
            kx_ref[0, :, half:] = oy.astype(BF16)


def _att_in(x, mod, gain, w, q_gain_a, q_gain_b, k_gain_a, k_gain_b, cos, sin):
    b, n, d = x.shape
    tm = ATT_TOK_TILE
    tq = ATT_Q_TILE
    hd = ATT_HD
    hw = V7X_MXU_DIM
    assert ATT_KV_HEADS * hd == hw
    return pl.pallas_call(
        functools.partial(_att_in_kernel, d=d),
        grid=(b, n // tm),
        in_specs=[
            pl.BlockSpec((1, tm, d), lambda bi, i: (bi, i, 0)),
            _mod_spec(mod),
            _resident(gain.shape),
            _resident(w.shape),
            _resident(q_gain_a.shape),
            _resident(q_gain_b.shape),
            _resident(k_gain_a.shape),
            _resident(k_gain_b.shape),
            pl.BlockSpec((tm, hd), lambda bi, i: (i, 0)),
            pl.BlockSpec((tm, hd), lambda bi, i: (i, 0)),
        ],
        out_specs=[
            pl.BlockSpec((1, d // hw, tm // tq, hw, tq), lambda bi, i: (bi, 0, i, 0, 0)),
            pl.BlockSpec((1, tm, hw), lambda bi, i: (bi, i, 0)),
            pl.BlockSpec((1, ATT_KV_HEADS, hd + ONES_ROWS, tm), lambda bi, i: (bi, 0, 0, i)),
        ],
        out_shape=[
            jax.ShapeDtypeStruct((b, d // hw, n // tq, hw, tq), BF16),
            jax.ShapeDtypeStruct((b, n, hw), BF16),
            jax.ShapeDtypeStruct((b, ATT_KV_HEADS, hd + ONES_ROWS, n), BF16),
        ],
        scratch_shapes=[pltpu.VMEM((tm, d), BF16)],
        compiler_params=_params("arbitrary", "arbitrary"),
        name="att_in_proj",
    )(x, mod.table, gain, w, q_gain_a, q_gain_b, k_gain_a, k_gain_b, cos, sin)


def _attention_kernel(qt_ref, kx_ref, vt_ref, o_ref, s_scr, p_scr, *, group, tq):
    hd = ATT_HD
    nt = qt_ref.shape[2]
    hw = qt_ref.shape[3]
    row_group = (lax.broadcasted_iota(jnp.int32, (hw, 1), 0) // (hd // 2)) % ATT_KV_HEADS
    mine = row_group == pl.program_id(1)

    def rows(i):
        return pl.ds(pl.multiple_of(i * tq, tq), tq)

    def scores(i, g):
        slab = qt_ref[0, g, i]
        s_scr[g] = _dot(kx_ref[0], jnp.where(mine, slab, jnp.zeros_like(slab)))

    def probs(g):
        m = jnp.max(s_scr[g], axis=0, keepdims=True)
        p_scr[g % 2] = jnp.exp2(s_scr[g] - m).astype(BF16)

    def weighted(i, g):
        acc = _dot(vt_ref[0, 0], p_scr[g % 2])
        out_t = acc[:hd] / acc[hd:hd + 1]
        o_ref[0, 0, rows(i), g * hd:(g + 1) * hd] = out_t.T.astype(BF16)

    for g in range(ATT_LOOKAHEAD):
        scores(0, g)

    def body(i, carry):
        for g in range(group):
            probs(g)
            ahead = g + ATT_LOOKAHEAD
            if ahead < group:
                scores(i, ahead)
            else:
                scores(jnp.minimum(i + 1, nt - 1), ahead - group)
            weighted(i, g)
        return carry

    lax.fori_loop(0, nt, body, 0, unroll=ATT_TILE_UNROLL)


def _attention(qt, kx, vt):
    b, group, nt, hw, tq = qt.shape
    n = kx.shape[1]
    hd = ATT_HD
    assert group % 2 == 0
    return pl.pallas_call(
        functools.partial(_attention_kernel, group=group, tq=tq),
        grid=(b, ATT_KV_HEADS),
        in_specs=[
            pl.BlockSpec((1, group, nt, hw, tq), lambda bi, g: (bi, 0, 0, 0, 0),
                         pipeline_mode=pl.Buffered(1)),
            pl.BlockSpec((1, n, hw), lambda bi, g: (bi, 0, 0)),
            pl.BlockSpec((1, 1, vt.shape[2], n), lambda bi, g: (bi, g, 0, 0)),
        ],
        out_specs=pl.BlockSpec((1, 1, n, group * hd), lambda bi, g: (bi, g, 0, 0)),
        out_shape=jax.ShapeDtypeStruct((b, ATT_KV_HEADS, n, group * hd), BF16),
        scratch_shapes=[pltpu.VMEM((group, n, tq), F32), pltpu.VMEM((2, n, tq), BF16)],
        compiler_params=_params("arbitrary", "arbitrary"),
        name="gqa_attention",
    )(qt, kx, vt)


def _rope_tables(n, head_dim, copies):
    rows = n // GRID_W
    nf = head_dim // 4
    inv = ROPE_THETA ** (-jnp.arange(nf, dtype=F32) / nf)
    ang_r = jnp.arange(rows, dtype=F32)[:, None] * inv[None, :]
    ang_c = jnp.arange(GRID_W, dtype=F32)[:, None] * inv[None, :]

    def table(fn):
        per_row = jnp.repeat(fn(ang_r), GRID_W, axis=0)
        per_col = jnp.tile(fn(ang_c), (rows, 1))
        return jnp.concatenate([per_row, per_col] * copies, axis=-1)

    return table(jnp.cos), table(jnp.sin)


def _trunk(x, mod_table, first_row, p):
    mod = ModRows(mod_table, 0, first_row)
    q, kt, v, gate = _ret_in(x, mod, p["norm1_g"][0], p["ret_w_in"], p["ret_cos"], p["ret_sin"])
    y = _retention(q, kt, v, p["ret_dec"])
    x = _out_mlp(x, y, gate, mod, p["ret_w_out"], p["norm2_g"][0], p["mlp_w1"][0], p["mlp_w2"][0],
                 p["final_g"], final=False)
    mod = ModRows(mod_table, 1, first_row)
    qt, kx, vt = _att_in(x, mod, p["norm1_g"][1], p["att_w_in"], p["att_q_gain_a"], p["att_q_gain_b"],
                         p["att_k_gain_a"], p["att_k_gain_b"], p["att_cos"], p["att_sin"])
    y = _attention(qt, kx, vt)
    return _out_mlp(x, y, None, mod, p["att_w_out"], p["norm2_g"][1], p["mlp_w1"][1], p["mlp_w2"][1],
                    p["final_g"], final=True)


def kernel(x_prompt, x_sample, c_prompt, c_sample, mod_w, mod_b, norm1_g, norm2_g, ret_w_in, ret_decay,
           ret_w_out, att_w_in, att_q_gain, att_k_gain, att_w_out, mlp_w1, mlp_w2, final_g):
    depth, d, _ = mod_w.shape
    assert depth == 2 and ret_w_in.shape[0] == 1 and att_w_in.shape[0] == 1
    bp, n, _ = x_prompt.shape
    assert x_sample.shape[1] == n

    c_all = jnp.concatenate([c_prompt, c_sample], axis=0)
    pad = (-c_all.shape[0]) % 16
    c_all = jnp.pad(c_all, ((0, pad), (0, 0)))
    mod = _adaln(c_all, mod_w, mod_b).reshape(depth, -1, N_MOD, d)

    dk = d // RET_HEADS
    dv = 2 * d // RET_HEADS
    w_ret = ret_w_in[0]
    ret_qk = w_ret[:, :2 * d].reshape(d, 2 * RET_HEADS, 2, 2, dk // 4).transpose(0, 1, 3, 2, 4)
    ret_w = jnp.concatenate([ret_qk.reshape(d, 2 * d), w_ret[:, 2 * d:]], axis=1)
    w_att = att_w_in[0]
    group = d // ATT_HD // ATT_KV_HEADS
    att_q = w_att[:, :d].reshape(d, ATT_KV_HEADS, group, 2, 2, ATT_HD // 4).transpose(0, 2, 4, 1, 3, 5)
    att_k = w_att[:, d:d + ATT_KV_HEADS * ATT_HD].reshape(d, ATT_KV_HEADS, 2, 2, ATT_HD // 4)
    att_k = att_k.transpose(0, 3, 1, 2, 4)
    att_w = jnp.concatenate([att_q.reshape(d, d), att_k.reshape(d, ATT_KV_HEADS * ATT_HD),
                             w_att[:, d + ATT_KV_HEADS * ATT_HD:]], axis=1)

    def slab_gain(gain, half):
        part = gain.reshape(2, 2, ATT_HD // 4)[:, half, :].reshape(-1)
        return jnp.tile(part, 2).reshape(1, ATT_HD)

    ret_cos, ret_sin = _rope_tables(n, dk, 1)
    att_cos, att_sin = _rope_tables(n, ATT_HD, 2)
    p = {
        "norm1_g": norm1_g.reshape(depth, 1, d),
        "norm2_g": norm2_g.reshape(depth, 1, d),
        "final_g": final_g.reshape(1, d),
        "ret_w_in": ret_w.astype(BF16),
        "ret_w_out": ret_w_out[0].astype(BF16),
        "att_w_in": att_w.astype(BF16),
        "att_w_out": att_w_out[0].astype(BF16),
        "att_q_gain_a": slab_gain(att_q_gain[0], 0),
        "att_q_gain_b": slab_gain(att_q_gain[0], 1),
        "att_k_gain_a": slab_gain(att_k_gain[0], 0),
        "att_k_gain_b": slab_gain(att_k_gain[0], 1),
        "mlp_w1": mlp_w1.astype(BF16),
        "mlp_w2": mlp_w2.astype(BF16),
        "ret_cos": ret_cos,
        "ret_sin": ret_sin,
        "att_cos": att_cos,
        "att_sin": att_sin,
        "ret_dec": jnp.pad(jnp.broadcast_to(ret_decay[0].T[:, :, None], (RET_HEADS, 2, dv)),
                           ((0, 0), (0, 6), (0, 0))),
    }
    y_prompt = _trunk(x_prompt, mod, 0, p)
    y_sample = _trunk(x_sample, mod, bp, p)
    return (y_prompt, y_sample)
```
